```python
import math
import jax, jax.numpy as jnp
from jax import lax
import numpy as np

D_MODEL = 1024
BATCH = 8
SEQ = 2048
DEPTH = 2
DEC_BATCH = 32
DEC_SEQ = 8
PAST_LEN = 8192
PAGE_SIZE = 128

N_A_LAYERS = DEPTH // 2
N_B_LAYERS = DEPTH - N_A_LAYERS
POOL_WINDOWS = (2, 4, 8, 16)
N_POOL_GROUPS = len(POOL_WINDOWS)
POOL_GROUP_DIM = D_MODEL // N_POOL_GROUPS
POOL_HIST = max(POOL_WINDOWS) - 1
HEAD_DIM = 64
N_HEADS = D_MODEL // HEAD_DIM
D_FF = 4 * D_MODEL
Q_BLOCK = 128
LN_EPS = 1e-5
DN_ALPHA = float((2 * DEPTH) ** 0.25)
DN_BETA = float((8 * DEPTH) ** -0.25)
BREAK_BIAS_LO = 4.0
BREAK_BIAS_HI = 9.0

kernel_name = "yoco_pool_stickbreak_decoder_step"


def _layer_norm(x, g, b):
    xf = x.astype(jnp.float32)
    mu = jnp.mean(xf, axis=-1, keepdims=True)
    var = jnp.mean(jnp.square(xf - mu), axis=-1, keepdims=True)
    y = (xf - mu) * lax.rsqrt(var + LN_EPS) * g.astype(jnp.float32) + b.astype(jnp.float32)
    return y.astype(x.dtype)


def _ada(c, w, b):
    m = jax.nn.silu(c) @ w + b
    return [t[:, None, :] for t in jnp.split(m, 6, axis=-1)]


def _pool_mixer(h, hist, pos0, w_pool, pool_scale):
    T = h.shape[1]
    ext = jnp.concatenate([hist, h], axis=1)
    extf = ext.astype(jnp.float32)
    cs = jnp.cumsum(extf, axis=1)
    cs0 = jnp.concatenate([jnp.zeros_like(cs[:, :1]), cs], axis=1)
    pos = pos0 + jnp.arange(T)
    outs = []
    for g, w in enumerate(POOL_WINDOWS):
        sl = slice(g * POOL_GROUP_DIM, (g + 1) * POOL_GROUP_DIM)
        hi = cs0[:, POOL_HIST + 1:POOL_HIST + 1 + T, sl]
        lo = cs0[:, POOL_HIST + 1 - w:POOL_HIST + 1 - w + T, sl]
        cnt = jnp.minimum(w, pos + 1).astype(jnp.float32)[None, :, None]
        d = ((hi - lo) / cnt - extf[:, POOL_HIST:, sl]).astype(h.dtype)
        outs.append(d @ w_pool[g])
    out = jnp.concatenate(outs, axis=-1) * pool_scale
    return out, ext[:, -POOL_HIST:]


def _stick_breaking(q, k, v, q_pos0, break_bias):
    T = q.shape[1]
    L = k.shape[1]
    blk = min(Q_BLOCK, T)
    scale = HEAD_DIM ** -0.5
    bias = break_bias.astype(jnp.float32)[None, :, None, None]
    outs = []
    for start in range(0, T, blk):
        qb = q[:, start:start + blk]
        nb = qb.shape[1]
        kv_len = min(L, q_pos0 + start + nb)
        kb = k[:, :kv_len]
        vb = v[:, :kv_len]
        z = jnp.einsum('bqhd,bkhd->bhqk', qb, kb, preferred_element_type=jnp.float32) * scale + bias
        qpos = q_pos0 + start + jnp.arange(nb)
        kpos = jnp.arange(kv_len)
        mask = kpos[None, :] < qpos[:, None]
        log_beta = jax.nn.log_sigmoid(z)
        log_keep = jnp.where(mask, log_beta - z, 0.0)
        rest = lax.cumsum(log_keep, axis=3, reverse=True) - log_keep
        a = jnp.where(mask, jnp.exp(log_beta + rest), 0.0)
        outs.append(jnp.einsum('bhqk,bkhd->bqhd', a.astype(vb.dtype), vb))
    return jnp.concatenate(outs, axis=1)


def setup_inputs(seed: int = 0) -> dict:
    key = jax.random.key(seed)
    ks = jax.random.split(key, 20)
    n_pages = PAST_LEN // PAGE_SIZE
    n_used = DEC_BATCH * n_pages
    n_pool_pages = n_used + max(1, n_used // 4)
    f32 = jnp.float32
    x_prompt = jax.random.normal(ks[0], (BATCH, SEQ, D_MODEL), f32)
    x_sample = jax.random.normal(ks[1], (DEC_BATCH, DEC_SEQ, D_MODEL), f32)
    cache_k = jax.random.normal(ks[2], (n_pool_pages, PAGE_SIZE, N_HEADS, HEAD_DIM), f32)
    cache_v = jax.random.normal(ks[3], (n_pool_pages, PAGE_SIZE, N_HEADS, HEAD_DIM), f32) * DN_BETA
    state_pool = jax.random.normal(ks[4], (N_A_LAYERS, DEC_BATCH, POOL_HIST, D_MODEL), f32)
    page_table = jax.random.permutation(ks[5], n_pool_pages)[:n_used].reshape(DEC_BATCH, n_pages).astype(jnp.int32)
    c_prompt = jax.random.normal(ks[6], (BATCH, D_MODEL), f32)
    c_sample = jax.random.normal(ks[7], (DEC_BATCH, D_MODEL), f32)
    w_ada = jax.random.normal(ks[8], (DEPTH, D_MODEL, 6 * D_MODEL), f32) * (0.5 * D_MODEL ** -0.5)
    b_ada = jax.random.normal(ks[9], (DEPTH, 6 * D_MODEL), f32) * 0.02
    ln_g = 1.0 + 0.05 * jax.random.normal(ks[10], (DEPTH, 2, D_MODEL), f32)
    ln_b = 0.02 * jax.random.normal(ks[11], (DEPTH, 2, D_MODEL), f32)
    w_pool = jax.random.normal(ks[12], (N_A_LAYERS, N_POOL_GROUPS, POOL_GROUP_DIM, POOL_GROUP_DIM), f32) * (POOL_GROUP_DIM ** -0.5 * DN_BETA)
    pool_scale = 1.0 + 0.1 * jax.random.normal(ks[13], (N_A_LAYERS, D_MODEL), f32)
    w_up = jax.random.normal(ks[14], (DEPTH, D_MODEL, D_FF), f32) * D_MODEL ** -0.5
    w_down = jax.random.normal(ks[15], (DEPTH, D_FF, D_MODEL), f32) * (D_FF ** -0.5 * DN_BETA)
    kv_scale = jnp.concatenate([jnp.ones((D_MODEL,), f32), jnp.full((D_MODEL,), DN_BETA, f32)])
    w_kv = jax.random.normal(ks[16], (D_MODEL, 2 * D_MODEL), f32) * D_MODEL ** -0.5 * kv_scale
    w_q = jax.random.normal(ks[17], (N_B_LAYERS, D_MODEL, D_MODEL), f32) * D_MODEL ** -0.5
    w_o = jax.random.normal(ks[18], (N_B_LAYERS, D_MODEL, D_MODEL), f32) * (D_MODEL ** -0.5 * DN_BETA)
    b_break = (-jnp.linspace(BREAK_BIAS_LO, BREAK_BIAS_HI, N_HEADS, dtype=f32)[None, :]
               + 0.1 * jax.random.normal(ks[19], (N_B_LAYERS, N_HEADS), f32))
    return {"x_prompt": x_prompt, "x_sample": x_sample, "cache_k": cache_k, "cache_v": cache_v,
            "state_pool": state_pool, "page_table": page_table, "c_prompt": c_prompt, "c_sample": c_sample,
            "w_ada": w_ada, "b_ada": b_ada, "ln_g": ln_g, "ln_b": ln_b, "w_pool": w_pool,
            "pool_scale": pool_scale, "w_up": w_up, "w_down": w_down, "w_kv": w_kv, "w_q": w_q, "w_o": w_o,
            "b_break": b_break}


def reference(x_prompt, x_sample, cache_k, cache_v, state_pool, page_table, c_prompt, c_sample,
              w_ada, b_ada, ln_g, ln_b, w_pool, pool_scale, w_up, w_down, w_kv, w_q, w_o, b_break):
    def run_group(x, c, pool_hist, pos0, past_k, past_v):
        bsz, T, _ = x.shape
        new_hist = []
        keys = vals = k_new = v_new = None
        for layer in range(DEPTH):
            sh1, sc1, g1, sh2, sc2, g2 = _ada(c, w_ada[layer], b_ada[layer])
            if layer == N_A_LAYERS:
                kv = x @ w_kv
                k_new = kv[..., :D_MODEL].reshape(bsz, T, N_HEADS, HEAD_DIM)
                v_new = kv[..., D_MODEL:].reshape(bsz, T, N_HEADS, HEAD_DIM)
                if past_k is None:
                    keys, vals = k_new, v_new
                else:
                    keys = jnp.concatenate([past_k, k_new], axis=1)
                    vals = jnp.concatenate([past_v, v_new], axis=1)
            h = x * (1.0 + sc1) + sh1
            if layer < N_A_LAYERS:
                mix, hist = _pool_mixer(h, pool_hist[layer], pos0, w_pool[layer], pool_scale[layer])
                new_hist.append(hist)
            else:
                j = layer - N_A_LAYERS
                q = (h @ w_q[j]).reshape(bsz, T, N_HEADS, HEAD_DIM)
                o = _stick_breaking(q, keys, vals, pos0, b_break[j])
                mix = o.reshape(bsz, T, D_MODEL) @ w_o[j]
            x = _layer_norm(DN_ALPHA * x + g1 * mix, ln_g[layer, 0], ln_b[layer, 0])
            h = x * (1.0 + sc2) + sh2
            f = jnp.square(jax.nn.relu(h @ w_up[layer])) @ w_down[layer]
            x = _layer_norm(DN_ALPHA * x + g2 * f, ln_g[layer, 1], ln_b[layer, 1])
        return x, jnp.stack(new_hist, axis=0), k_new, v_new

    zero_hist = jnp.zeros((N_A_LAYERS, x_prompt.shape[0], POOL_HIST, D_MODEL), x_prompt.dtype)
    y_prompt, pool_prompt, k_prompt, v_prompt = run_group(x_prompt, c_prompt, zero_hist, 0, None, None)

    dec_b, n_pages = page_table.shape
    past_len = n_pages * PAGE_SIZE
    past_k = cache_k[page_table].reshape(dec_b, past_len, N_HEADS, HEAD_DIM)
    past_v = cache_v[page_table].reshape(dec_b, past_len, N_HEADS, HEAD_DIM)
    y_sample, pool_sample, k_sample, v_sample = run_group(x_sample, c_sample, state_pool, past_len, past_k, past_v)

    return (y_prompt, y_sample, k_prompt, v_prompt, pool_prompt, k_sample, v_sample, pool_sample)
```

```python
import functools

import jax
import jax.numpy as jnp
from jax import lax
from jax.experimental import pallas as pl
from jax.experimental.pallas import tpu as pltpu

F32 = jnp.float32
BF16 = jnp.bfloat16

D_MODEL = 1024
DEPTH = 2
N_HEADS = 16
HEAD_DIM = 64
HEADS_PER_SLAB = 2
SLAB = HEADS_PER_SLAB * HEAD_DIM
N_SLABS = N_HEADS // HEADS_PER_SLAB
D_FF = 4 * D_MODEL
POOL_WINDOWS = (2, 4, 8, 16)
POOL_GROUP_DIM = D_MODEL // len(POOL_WINDOWS)
POOL_HIST = max(POOL_WINDOWS) - 1
HIST_ROWS = POOL_HIST + 1
LN_EPS = 1e-5
DN_ALPHA = float((2 * DEPTH) ** 0.25)
ATTN_SCALE = HEAD_DIM ** -0.5
PAGE_SIZE = 128

VMEM_LIMIT_BYTES = 56 * 1024 * 1024
ADA_TILE_N = 1536
MLP_TILE_M = 512
MLP_TILE_F = 512
ATTN_BLOCK = 256
DECODE_PAGES_PER_STEP = 4


def _params(*sem):
    return pltpu.CompilerParams(dimension_semantics=sem, vmem_limit_bytes=VMEM_LIMIT_BYTES)


def _layer_norm(y, g, b):
    mu = jnp.mean(y, axis=-1, keepdims=True)
    yc = y - mu
    var = jnp.mean(yc * yc, axis=-1, keepdims=True)
    return yc * lax.rsqrt(var + LN_EPS) * g + b


def _resident(shape, index_map):
    return pl.BlockSpec(shape, index_map, pipeline_mode=pl.Buffered(1))


def _ada_kernel(c_ref, w_ref, b_ref, o_ref):
    c = c_ref[...]
    s = c * jax.nn.sigmoid(c)
    o_ref[0] = jnp.dot(s.astype(BF16), w_ref[0].astype(BF16), preferred_element_type=F32) + b_ref[0]


def _ada(c_all, w_ada, b_ada):
    n_layers, d, n = w_ada.shape
    m = c_all.shape[0]
    return pl.pallas_call(
        _ada_kernel,
        grid=(n_layers, n // ADA_TILE_N),
        in_specs=[
            pl.BlockSpec((m, d), lambda l, j: (0, 0)),
            pl.BlockSpec((1, d, ADA_TILE_N), lambda l, j: (l, 0, j)),
            pl.BlockSpec((1, 1, ADA_TILE_N), lambda l, j: (l, 0, j)),
        ],
        out_specs=pl.BlockSpec((1, m, ADA_TILE_N), lambda l, j: (l, 0, j)),
        out_shape=jax.ShapeDtypeStruct((n_layers, m, n), F32),
        compiler_params=_params("parallel", "parallel"),
        name="ada",
    )(c_all, w_ada, b_ada.reshape(n_layers, 1, n))


def _mod_spec(rows, chunk, row_off):
    return pl.BlockSpec((1, rows, D_MODEL), lambda b, t: (b + row_off, 0, chunk))


def _mixer_kernel(pos0, tile_t, x_ref, hist_ref, sh_ref, sc_ref, g_ref, wp_ref, ps_ref,
                  lng_ref, lnb_ref, o_ref, ho_ref, ext_ref):
    t = pl.program_id(1)

    @pl.when(t == 0)
    def _():
        ext_ref[0:HIST_ROWS, :] = hist_ref[0]

    x = x_ref[0]
    h = x * (1.0 + sc_ref[0]) + sh_ref[0]
    ext_ref[HIST_ROWS:HIST_ROWS + tile_t, :] = h
    pos = pos0 + t * tile_t + lax.broadcasted_iota(jnp.int32, (tile_t, 1), 0)
    outs = []
    for g, w in enumerate(POOL_WINDOWS):
        lo, hi = g * POOL_GROUP_DIM, (g + 1) * POOL_GROUP_DIM
        hg = ext_ref[HIST_ROWS:HIST_ROWS + tile_t, lo:hi]
        wsum = hg
        for i in range(1, w):
            wsum = wsum + ext_ref[HIST_ROWS - i:HIST_ROWS - i + tile_t, lo:hi]
        cnt = jnp.minimum(w, pos + 1).astype(F32)
        d = wsum / cnt - hg
        outs.append(jnp.dot(d.astype(BF16), wp_ref[g], preferred_element_type=F32))
    mix = jnp.concatenate(outs, axis=-1) * ps_ref[...]
    y = DN_ALPHA * x + g_ref[0] * mix
    o_ref[0] = _layer_norm(y, lng_ref[0], lnb_ref[0])
    tail = ext_ref[tile_t:tile_t + HIST_ROWS, :]
    ho_ref[0] = tail
    ext_ref[0:HIST_ROWS, :] = tail


def _mixer(x, hist16, mod, row_off, pos0, wp_bf, pool_scale, ln_g, ln_b, ln_idx):
    bsz, seq, d = x.shape
    tile_t = min(seq, 256)
    return pl.pallas_call(
        functools.partial(_mixer_kernel, pos0, tile_t),
        grid=(bsz, seq // tile_t),
        in_specs=[
            pl.BlockSpec((1, tile_t, d), lambda b, t: (b, t, 0)),
            pl.BlockSpec((1, HIST_ROWS, d), lambda b, t: (b, 0, 0)),
            _mod_spec(1, 0, row_off), _mod_spec(1, 1, row_off), _mod_spec(1, 2, row_off),
            pl.BlockSpec(wp_bf.shape, lambda b, t: (0, 0, 0)),
            pl.BlockSpec((1, d), lambda b, t: (0, 0)),
            pl.BlockSpec((1, 1, d), lambda b, t: (ln_idx, 0, 0)),
            pl.BlockSpec((1, 1, d), lambda b, t: (ln_idx, 0, 0)),
        ],
        out_specs=[
            pl.BlockSpec((1, tile_t, d), lambda b, t: (b, t, 0)),
            pl.BlockSpec((1, HIST_ROWS, d), lambda b, t: (b, 0, 0)),
        ],
        out_shape=[jax.ShapeDtypeStruct((bsz, seq, d), F32),
                   jax.ShapeDtypeStruct((bsz, HIST_ROWS, d), F32)],
        scratch_shapes=[pltpu.VMEM((HIST_ROWS + tile_t, d), F32)],
        compiler_params=_params("parallel", "arbitrary"),
        name="mixer",
    )(x, hist16, mod, mod, mod, wp_bf, pool_scale, ln_g, ln_b)


def _mlp_kernel(has_oproj, *refs):
    if has_oproj:
        (x_ref, o_ref, g1_ref, wo_ref, lng1_ref, lnb1_ref,
         sh_ref, sc_ref, g_ref, wu_ref, wd_ref, lng_ref, lnb_ref, out_ref, acc_ref) = refs
        mix = jnp.dot(o_ref[0], wo_ref[...], preferred_element_type=F32)
        x = _layer_norm(DN_ALPHA * x_ref[0] + g1_ref[0] * mix, lng1_ref[0], lnb1_ref[0])
    else:
        (x_ref, sh_ref, sc_ref, g_ref, wu_ref, wd_ref, lng_ref, lnb_ref, out_ref, acc_ref) = refs
        x = x_ref[0]
    h = (x * (1.0 + sc_ref[0]) + sh_ref[0]).astype(BF16)
    for c in range(D_FF // MLP_TILE_F):
        lo, hi = c * MLP_TILE_F, (c + 1) * MLP_TILE_F
        u = jnp.dot(h, wu_ref[:, lo:hi], preferred_element_type=F32)
        u = jnp.maximum(u, 0.0)
        contrib = jnp.dot((u * u).astype(BF16), wd_ref[lo:hi, :], preferred_element_type=F32)
        if c == 0:
            acc_ref[...] = contrib
        else:
            acc_ref[...] += contrib
    y = DN_ALPHA * x + g_ref[0] * acc_ref[...]
    out_ref[0] = _layer_norm(y, lng_ref[0], lnb_ref[0])


def _mlp(x, mod, mod_rows, row_off, wu_bf, wd_bf, ln_g, ln_b, ln_idx, oproj=None):
    grp, rows, d = x.shape
    tile_m = min(rows, MLP_TILE_M)
    xspec = pl.BlockSpec((1, tile_m, d), lambda b, t: (b, t, 0))
    lnspec = lambda idx: pl.BlockSpec((1, 1, d), lambda b, t: (idx, 0, 0))
    ins, specs = [x], [xspec]
    if oproj is not None:
        o_bf, wo_bf, ln_idx1 = oproj
        ins += [o_bf, mod, wo_bf, ln_g, ln_b]
        specs += [xspec, _mod_spec(mod_rows, 2, row_off), _resident((d, d), lambda b, t: (0, 0)),
                  lnspec(ln_idx1), lnspec(ln_idx1)]
    ins += [mod, mod, mod, wu_bf, wd_bf, ln_g, ln_b]
    specs += [_mod_spec(mod_rows, 3, row_off), _mod_spec(mod_rows, 4, row_off), _mod_spec(mod_rows, 5, row_off),
              _resident((d, D_FF), lambda b, t: (0, 0)), _resident((D_FF, d), lambda b, t: (0, 0)),
              lnspec(ln_idx), lnspec(ln_idx)]
    return pl.pallas_call(
        functools.partial(_mlp_kernel, oproj is not None),
        grid=(grp, rows // tile_m),
        in_specs=specs,
        out_specs=xspec,
        out_shape=jax.ShapeDtypeStruct((grp, rows, d), F32),
        scratch_shapes=[pltpu.VMEM((tile_m, d), F32)],
        compiler_params=_params("parallel", "parallel"),
        name="mlp_oproj" if oproj is not None else "mlp",
    )(*ins)


def _kvq_kernel(x_ref, sh_ref, sc_ref, wkv_ref, wq_ref, k_ref, v_ref, kb_ref, vb_ref, q_ref):
    x = x_ref[0]
    kv = jnp.dot(x.astype(BF16), wkv_ref[...], preferred_element_type=F32)
    k, v = kv[:, :D_MODEL], kv[:, D_MODEL:]
    k_ref[0] = k
    v_ref[0] = v
    kb_ref[0] = k.astype(BF16)
    vb_ref[0] = v.astype(BF16)
    h = (x * (1.0 + sc_ref[0]) + sh_ref[0]).astype(BF16)
    q_ref[0] = jnp.dot(h, wq_ref[...], preferred_element_type=F32).astype(q_ref.dtype)


def _kvq(x, mod, mod_rows, row_off, wkv_bf, wq_bf, q_dtype):
    grp, rows, d = x.shape
    tile_m = min(rows, MLP_TILE_M)
    xspec = pl.BlockSpec((1, tile_m, d), lambda b, t: (b, t, 0))
    sds = lambda dt: jax.ShapeDtypeStruct((grp, rows, d), dt)
    return pl.pallas_call(
        _kvq_kernel,
        grid=(grp, rows // tile_m),
        in_specs=[xspec, _mod_spec(mod_rows, 0, row_off), _mod_spec(mod_rows, 1, row_off),
                  _resident((d, 2 * d), lambda b, t: (0, 0)), _resident((d, d), lambda b, t: (0, 0))],
        out_specs=[xspec] * 5,
        out_shape=[sds(F32), sds(F32), sds(BF16), sds(BF16), sds(q_dtype)],
        compiler_params=_params("parallel", "parallel"),
        name="kvq",
    )(x, mod, mod, wkv_bf, wq_bf)


def _stick_tile(s, bias, r_carry, mask, tri):
    z = s * ATTN_SCALE + bias
    l1p = jnp.log1p(jnp.exp(-jnp.abs(z)))
    log_beta = jnp.minimum(z, 0.0) - l1p
    log_keep = -jnp.maximum(z, 0.0) - l1p
    if mask is not None:
        log_keep = jnp.where(mask, log_keep, 0.0)
    hi = log_keep.astype(BF16)
    lo = (log_keep - hi.astype(F32)).astype(BF16)
    rest = (jnp.dot(hi, tri, preferred_element_type=F32) + jnp.dot(lo, tri, preferred_element_type=F32))
    a = jnp.exp(log_beta + rest + r_carry)
    if mask is not None:
        a = jnp.where(mask, a, 0.0)
    return a, jnp.sum(log_keep, axis=-1, keepdims=True)


def _triangle(n):
    r = lax.broadcasted_iota(jnp.int32, (n, n), 0)
    c = lax.broadcasted_iota(jnp.int32, (n, n), 1)
    return jnp.where(r > c, 1.0, 0.0).astype(BF16)


def _attn_kernel(blk, bias_ref, q_ref, k_ref, v_ref, o_ref):
    slab = pl.program_id(1)
    i = pl.program_id(2)
    q = q_ref[0]
    lane = lax.broadcasted_iota(jnp.int32, (1, SLAB), 1)
    row = lax.broadcasted_iota(jnp.int32, (blk, blk), 0)
    col = lax.broadcasted_iota(jnp.int32, (blk, blk), 1)
    causal = col < row
    tri = _triangle(blk)
    accs = []
    for hh in range(HEADS_PER_SLAB):
        in_head = (lane >= hh * HEAD_DIM) & (lane < (hh + 1) * HEAD_DIM)
        qm = jnp.where(in_head, q, jnp.zeros_like(q))
        bias = bias_ref[slab * HEADS_PER_SLAB + hh]

        def tile(j, r_carry, acc, mask):
            start = pl.multiple_of(j * blk, blk)
            kb = k_ref[0, pl.ds(start, blk), :]
            vb = v_ref[0, pl.ds(start, blk), :]
            s = lax.dot_general(qm, kb, (((1,), (1,)), ((), ())), preferred_element_type=F32)
            a, lk_sum = _stick_tile(s, bias, r_carry, mask, tri)
            acc = acc + jnp.dot(a.astype(BF16), vb, preferred_element_type=F32)
            return r_carry + lk_sum, acc

        r0, acc0 = tile(i, jnp.zeros((blk, 1), F32), jnp.zeros((blk, SLAB), F32), causal)

        def body(n, carry):
            return tile(i - 1 - n, carry[0], carry[1], None)

        _, acc = lax.fori_loop(0, i, body, (r0, acc0))
        accs.append(acc)
    o_ref[0] = jnp.where(lane < HEAD_DIM, accs[0], accs[1]).astype(o_ref.dtype)


def _attn_prompt(q_bf, k_bf, v_bf, bias):
    bsz, seq, d = q_bf.shape
    blk = min(seq, ATTN_BLOCK)
    qspec = pl.BlockSpec((1, blk, SLAB), lambda b, p, i: (b, i, p))
    kvspec = pl.BlockSpec((1, seq, SLAB), lambda b, p, i: (b, 0, p))
    return pl.pallas_call(
        functools.partial(_attn_kernel, blk),
        grid=(bsz, N_SLABS, seq // blk),
        in_specs=[pl.BlockSpec(memory_space=pltpu.SMEM), qspec, kvspec, kvspec],
        out_specs=qspec,
        out_shape=jax.ShapeDtypeStruct((bsz, seq, d), BF16),
        compiler_params=_params("parallel", "parallel", "arbitrary"),
        name="attn_prompt",
    )(bias, q_bf, k_bf, v_bf)


def _decode_kernel(n_pages_step, *refs):
    pt_ref = refs[0]
    q_ref, bias_ref, qpos_ref, kn_ref, vn_ref = refs[1:6]
    k_refs = refs[6:6 + n_pages_step]
    v_refs = refs[6 + n_pages_step:6 + 2 * n_pages_step]
    o_ref, r_ref, acc_ref = refs[6 + 2 * n_pages_step:]
    del pt_ref
    step = pl.program_id(1)
    t_new = q_ref.shape[1]
    rows = N_SLABS * HEADS_PER_SLAB * t_new
    slab_rows = HEADS_PER_SLAB * t_new

    q = q_ref[0]
    lane = lax.broadcasted_iota(jnp.int32, (1, SLAB), 1)
    qms = []
    for p in range(N_SLABS):
        qp = q[:, p * SLAB:(p + 1) * SLAB]
        parts = [jnp.where((lane >= hh * HEAD_DIM) & (lane < (hh + 1) * HEAD_DIM), qp, 0.0)
                 for hh in range(HEADS_PER_SLAB)]
        qms.append(jnp.concatenate(parts, axis=0).astype(BF16))
    bias = bias_ref[...]
    tri = _triangle(PAGE_SIZE)

    def page(kp, vp, mask):
        kpb, vpb = kp.astype(BF16), vp.astype(BF16)
        s = jnp.concatenate(
            [lax.dot_general(qms[p], kpb[:, p * SLAB:(p + 1) * SLAB], (((1,), (1,)), ((), ())),
                             preferred_element_type=F32) for p in range(N_SLABS)], axis=0)
        a, lk_sum = _stick_tile(s, bias, r_ref[...], mask, tri)
        ab = a.astype(BF16)
        for p in range(N_SLABS):
            acc_ref[p * slab_rows:(p + 1) * slab_rows, :] += jnp.dot(
                ab[p * slab_rows:(p + 1) * slab_rows, :], vpb[:, p * SLAB:(p + 1) * SLAB],
                preferred_element_type=F32)
        r_ref[...] += lk_sum

    @pl.when(step == 0)
    def _():
        r_ref[...] = jnp.zeros_like(r_ref)
        acc_ref[...] = jnp.zeros_like(acc_ref)
        key = lax.broadcasted_iota(jnp.int32, (rows, PAGE_SIZE), 1)
        page(kn_ref[0], vn_ref[0], key < qpos_ref[...])

    for g in reversed(range(n_pages_step)):
        page(k_refs[g][0], v_refs[g][0], None)

    @pl.when(step == pl.num_programs(1) - 1)
    def _():
        for p in range(N_SLABS):
            blk = acc_ref[p * slab_rows:(p + 1) * slab_rows, :]
            o_ref[0, :, p * SLAB:(p + 1) * SLAB] = jnp.where(
                lane < HEAD_DIM, blk[:t_new], blk[t_new:]).astype(o_ref.dtype)


def _attn_decode(q, k_new_pad, v_new_pad, cache_k, cache_v, page_table, bias_rows, qpos_rows):
    bsz, t_new, d = q.shape
    n_pages = page_table.shape[1]
    g = DECODE_PAGES_PER_STEP
    while n_pages % g:
        g //= 2
    n_steps = n_pages // g
    rows = N_HEADS * t_new

    def page_spec(slot):
        return pl.BlockSpec((1, PAGE_SIZE, d), lambda b, s, pt: (pt[b, (n_steps - 1 - s) * g + slot], 0, 0))

    per_b = lambda shape: pl.BlockSpec(shape, lambda b, s, pt: (b, 0, 0))
    grid_spec = pltpu.PrefetchScalarGridSpec(
        num_scalar_prefetch=1,
        grid=(bsz, n_steps),
        in_specs=[per_b((1, t_new, d)), pl.BlockSpec((rows, 1), lambda b, s, pt: (0, 0)),
                  pl.BlockSpec((rows, 1), lambda b, s, pt: (0, 0)),
                  per_b((1, PAGE_SIZE, d)), per_b((1, PAGE_SIZE, d))]
                 + [page_spec(slot) for slot in range(g)] * 2,
        out_specs=per_b((1, t_new, d)),
        scratch_shapes=[pltpu.VMEM((rows, 1), F32), pltpu.VMEM((rows, SLAB), F32)],
    )
    return pl.pallas_call(
        functools.partial(_decode_kernel, g),
        grid_spec=grid_spec,
        out_shape=jax.ShapeDtypeStruct((bsz, t_new, d), BF16),
        compiler_params=_params("parallel", "arbitrary"),
        name="attn_decode",
    )(page_table, q, bias_rows, qpos_rows, k_new_pad, v_new_pad, *([cache_k] * g), *([cache_v] * g))


def kernel(x_prompt, x_sample, cache_k, cache_v, state_pool, page_table, c_prompt, c_sample,
           w_ada, b_ada, ln_g, ln_b, w_pool, pool_scale, w_up, w_down, w_kv, w_q, w_o, b_break):
    bsz, seq, d = x_prompt.shape
    dec_b, dec_t, _ = x_sample.shape
    assert d == D_MODEL and w_ada.shape[0] == DEPTH == 2 and w_pool.shape[0] == 1 and w_q.shape[0] == 1
    past_len = page_table.shape[1] * PAGE_SIZE

    wu_bf, wd_bf = w_up.astype(BF16), w_down.astype(BF16)
    wkv_bf, wq_bf, wo_bf = w_kv.astype(BF16), w_q[0].astype(BF16), w_o[0].astype(BF16)
    wp_bf = w_pool[0].astype(BF16)
    ln_g4 = ln_g.reshape(DEPTH * 2, 1, d)
    ln_b4 = ln_b.reshape(DEPTH * 2, 1, d)

    ada = _ada(jnp.concatenate([c_prompt, c_sample], axis=0), w_ada, b_ada)
    mod_p = [ada[l].reshape(bsz + dec_b, 1, 6 * d) for l in range(DEPTH)]
    mod_s = [jnp.repeat(ada[l, bsz:], dec_t, axis=0)[None] for l in range(DEPTH)]
    xs_flat = lambda a: a.reshape(1, dec_b * dec_t, d)

    zero_hist = jnp.zeros((bsz, HIST_ROWS, d), F32)
    samp_hist = jnp.pad(state_pool[0], ((0, 0), (1, 0), (0, 0)))
    x1_p, hist_p = _mixer(x_prompt, zero_hist, mod_p[0], 0, 0, wp_bf, pool_scale, ln_g4, ln_b4, 0)
    x1_s, hist_s = _mixer(x_sample, samp_hist, mod_p[0], bsz, past_len, wp_bf, pool_scale, ln_g4, ln_b4, 0)
    x2_p = _mlp(x1_p, mod_p[0], 1, 0, wu_bf[0], wd_bf[0], ln_g4, ln_b4, 1)
    x2_s = _mlp(xs_flat(x1_s), mod_s[0], dec_b * dec_t, 0, wu_bf[0], wd_bf[0], ln_g4, ln_b4, 1)

    k_p, v_p, kb_p, vb_p, q_p = _kvq(x2_p, mod_p[1], 1, 0, wkv_bf, wq_bf, BF16)
    k_s, v_s, _, _, q_s = _kvq(x2_s, mod_s[1], dec_b * dec_t, 0, wkv_bf, wq_bf, F32)

    bias = b_break[0].astype(F32)
    o_p = _attn_prompt(q_p, kb_p, vb_p, bias)
    pad_new = lambda a: jnp.pad(a.reshape(dec_b, dec_t, d), ((0, 0), (0, PAGE_SIZE - dec_t), (0, 0)))
    o_s = _attn_decode(q_s.reshape(dec_b, dec_t, d), pad_new(k_s), pad_new(v_s),
                       cache_k.reshape(-1, PAGE_SIZE, d), cache_v.reshape(-1, PAGE_SIZE, d),
                       page_table, jnp.repeat(bias, dec_t)[:, None],
                       jnp.tile(jnp.arange(dec_t, dtype=jnp.int32), N_HEADS)[:, None])

    y_p = _mlp(x2_p, mod_p[1], 1, 0, wu_bf[1], wd_bf[1], ln_g4, ln_b4, 3, oproj=(o_p, wo_bf, 2))
    y_s = _mlp(x2_s, mod_s[1], dec_b * dec_t, 0, wu_bf[1], wd_bf[1], ln_g4, ln_b4, 3,
               oproj=(xs_flat(o_s), wo_bf, 2))

    heads = lambda a, b, t: a.reshape(b, t, N_HEADS, HEAD_DIM)
    return (y_p, y_s.reshape(dec_b, dec_t, d),
            heads(k_p, bsz, seq), heads(v_p, bsz, seq), hist_p[None, :, 1:],
            heads(k_s, dec_b, dec_t), heads(v_s, dec_b, dec_t), hist_s[None, :, 1:])
```

```python
import functools

import jax
import jax.numpy as jnp
from jax import lax
from jax.experimental import pallas as pl
from jax.experimental.pallas import tpu as pltpu

F32 = jnp.float32
BF16 = jnp.bfloat16

D_MODEL = 1024
DEPTH = 2
N_HEADS = 16
HEAD_DIM = 64
HEADS_PER_SLAB = 2
SLAB = HEADS_PER_SLAB * HEAD_DIM
N_SLABS = N_HEADS // HEADS_PER_SLAB
D_FF = 4 * D_MODEL
POOL_WINDOWS = (2, 4, 8, 16)
POOL_GROUP_DIM = D_MODEL // len(POOL_WINDOWS)
POOL_HIST = max(POOL_WINDOWS) - 1
HIST_ROWS = POOL_HIST + 1
LN_EPS = 1e-5
DN_ALPHA = float((2 * DEPTH) ** 0.25)
ATTN_SCALE = HEAD_DIM ** -0.5
PAGE_SIZE = 128

VMEM_LIMIT_BYTES = 56 * 1024 * 1024
ADA_TILE_N = 1536
MLP_TILE_M = 512
MLP_TILE_F = 512
ATTN_BLOCK = 256
ATTN_SLABS_PER_STEP = 4
DECODE_PAGES_PER_STEP = 8


def _params(*sem):
    return pltpu.CompilerParams(dimension_semantics=sem, vmem_limit_bytes=VMEM_LIMIT_BYTES)


def _layer_norm(y, g, b):
    mu = jnp.mean(y, axis=-1, keepdims=True)
    yc = y - mu
    var = jnp.mean(yc * yc, axis=-1, keepdims=True)
    return yc * lax.rsqrt(var + LN_EPS) * g + b


def _resident(shape, index_map):
    return pl.BlockSpec(shape, index_map, pipeline_mode=pl.Buffered(1))


def _ada_kernel(c_ref, w_ref, b_ref, o_ref):
    c = c_ref[...]
    s = c * jax.nn.sigmoid(c)
    o_ref[0] = jnp.dot(s.astype(BF16), w_ref[0].astype(BF16), preferred_element_type=F32) + b_ref[0]


def _ada(c_all, w_ada, b_ada):
    n_layers, d, n = w_ada.shape
    m = c_all.shape[0]
    return pl.pallas_call(
        _ada_kernel,
        grid=(n_layers, n // ADA_TILE_N),
        in_specs=[
            pl.BlockSpec((m, d), lambda l, j: (0, 0)),
            pl.BlockSpec((1, d, ADA_TILE_N), lambda l, j: (l, 0, j)),
            pl.BlockSpec((1, 1, ADA_TILE_N), lambda l, j: (l, 0, j)),
        ],
        out_specs=pl.BlockSpec((1, m, ADA_TILE_N), lambda l, j: (l, 0, j)),
        out_shape=jax.ShapeDtypeStruct((n_layers, m, n), F32),
        compiler_params=_params("parallel", "parallel"),
        name="ada",
    )(c_all, w_ada, b_ada.reshape(n_layers, 1, n))


def _mod_spec(rows, chunk, row_off):
    return pl.BlockSpec((1, rows, D_MODEL), lambda b, t: (b + row_off, 0, chunk))


def _mixer_kernel(pos0, tile_t, x_ref, hist_ref, sh_ref, sc_ref, g_ref, wp_ref, ps_ref,
                  lng_ref, lnb_ref, o_ref, ho_ref, ext_ref):
    t = pl.program_id(1)

    @pl.when(t == 0)
    def _():
        ext_ref[0:HIST_ROWS, :] = hist_ref[0]

    x = x_ref[0]
    h = x * (1.0 + sc_ref[0]) + sh_ref[0]
    ext_ref[HIST_ROWS:HIST_ROWS + tile_t, :] = h
    pos = pos0 + t * tile_t + lax.broadcasted_iota(jnp.int32, (tile_t, 1), 0)
    outs = []
    for g, w in enumerate(POOL_WINDOWS):
        lo, hi = g * POOL_GROUP_DIM, (g + 1) * POOL_GROUP_DIM
        hg = ext_ref[HIST_ROWS:HIST_ROWS + tile_t, lo:hi]
        wsum = hg
        for i in range(1, w):
            wsum = wsum + ext_ref[HIST_ROWS - i:HIST_ROWS - i + tile_t, lo:hi]
        cnt = jnp.minimum(w, pos + 1).astype(F32)
        d = wsum / cnt - hg
        outs.append(jnp.dot(d.astype(BF16), wp_ref[g], preferred_element_type=F32))
    mix = jnp.concatenate(outs, axis=-1) * ps_ref[...]
    y = DN_ALPHA * x + g_ref[0] * mix
    o_ref[0] = _layer_norm(y, lng_ref[0], lnb_ref[0])
    tail = ext_ref[tile_t:tile_t + HIST_ROWS, :]
    ho_ref[0] = tail
    ext_ref[0:HIST_ROWS, :] = tail


def _mixer(x, hist16, mod, row_off, pos0, wp_bf, pool_scale, ln_g, ln_b, ln_idx):
    bsz, seq, d = x.shape
    tile_t = min(seq, 256)
    return pl.pallas_call(
        functools.partial(_mixer_kernel, pos0, tile_t),
        grid=(bsz, seq // tile_t),
        in_specs=[
            pl.BlockSpec((1, tile_t, d), lambda b, t: (b, t, 0)),
            pl.BlockSpec((1, HIST_ROWS, d), lambda b, t: (b, 0, 0)),
            _mod_spec(1, 0, row_off), _mod_spec(1, 1, row_off), _mod_spec(1, 2, row_off),
            pl.BlockSpec(wp_bf.shape, lambda b, t: (0, 0, 0)),
            pl.BlockSpec((1, d), lambda b, t: (0, 0)),
            pl.BlockSpec((1, 1, d), lambda b, t: (ln_idx, 0, 0)),
            pl.BlockSpec((1, 1, d), lambda b, t: (ln_idx, 0, 0)),
        ],
        out_specs=[
            pl.BlockSpec((1, tile_t, d), lambda b, t: (b, t, 0)),
            pl.BlockSpec((1, HIST_ROWS, d), lambda b, t: (b, 0, 0)),
        ],
        out_shape=[jax.ShapeDtypeStruct((bsz, seq, d), F32),
                   jax.ShapeDtypeStruct((bsz, HIST_ROWS, d), F32)],
        scratch_shapes=[pltpu.VMEM((HIST_ROWS + tile_t, d), F32)],
        compiler_params=_params("parallel", "arbitrary"),
        name="mixer",
    )(x, hist16, mod, mod, mod, wp_bf, pool_scale, ln_g, ln_b)


def _mlp_kernel(has_oproj, *refs):
    if has_oproj:
        (x_ref, o_ref, g1_ref, wo_ref, lng1_ref, lnb1_ref,
         sh_ref, sc_ref, g_ref, wu_ref, wd_ref, lng_ref, lnb_ref, out_ref, acc_ref) = refs
        mix = jnp.dot(o_ref[0], wo_ref[...], preferred_element_type=F32)
        x = _layer_norm(DN_ALPHA * x_ref[0] + g1_ref[0] * mix, lng1_ref[0], lnb1_ref[0])
    else:
        (x_ref, sh_ref, sc_ref, g_ref, wu_ref, wd_ref, lng_ref, lnb_ref, out_ref, acc_ref) = refs
        x = x_ref[0]
    h = (x * (1.0 + sc_ref[0]) + sh_ref[0]).astype(BF16)
    for c in range(D_FF // MLP_TILE_F):
        lo, hi = c * MLP_TILE_F, (c + 1) * MLP_TILE_F
        u = jnp.dot(h, wu_ref[:, lo:hi], preferred_element_type=F32)
        u = jnp.maximum(u, 0.0)
        contrib = jnp.dot((u * u).astype(BF16), wd_ref[lo:hi, :], preferred_element_type=F32)
        if c == 0:
            acc_ref[...] = contrib
        else:
            acc_ref[...] += contrib
    y = DN_ALPHA * x + g_ref[0] * acc_ref[...]
    out_ref[0] = _layer_norm(y, lng_ref[0], lnb_ref[0])


def _mlp(x, mod, mod_rows, row_off, wu_bf, wd_bf, ln_g, ln_b, ln_idx, oproj=None):
    grp, rows, d = x.shape
    tile_m = min(rows, MLP_TILE_M)
    xspec = pl.BlockSpec((1, tile_m, d), lambda b, t: (b, t, 0))
    lnspec = lambda idx: pl.BlockSpec((1, 1, d), lambda b, t: (idx, 0, 0))
    ins, specs = [x], [xspec]
    if oproj is not None:
        o_bf, wo_bf, ln_idx1 = oproj
        ins += [o_bf, mod, wo_bf, ln_g, ln_b]
        specs += [xspec, _mod_spec(mod_rows, 2, row_off), _resident((d, d), lambda b, t: (0, 0)),
                  lnspec(ln_idx1), lnspec(ln_idx1)]
    ins += [mod, mod, mod, wu_bf, wd_bf, ln_g, ln_b]
    specs += [_mod_spec(mod_rows, 3, row_off), _mod_spec(mod_rows, 4, row_off), _mod_spec(mod_rows, 5, row_off),
              _resident((d, D_FF), lambda b, t: (0, 0)), _resident((D_FF, d), lambda b, t: (0, 0)),
              lnspec(ln_idx), lnspec(ln_idx)]
    return pl.pallas_call(
        functools.partial(_mlp_kernel, oproj is not None),
        grid=(grp, rows // tile_m),
        in_specs=specs,
        out_specs=xspec,
        out_shape=jax.ShapeDtypeStruct((grp, rows, d), F32),
        scratch_shapes=[pltpu.VMEM((tile_m, d), F32)],
        compiler_params=_params("parallel", "parallel"),
        name="mlp_oproj" if oproj is not None else "mlp",
    )(*ins)


def _kvq_kernel(x_ref, sh_ref, sc_ref, wkv_ref, wq_ref, k_ref, v_ref, kb_ref, vb_ref, q_ref):
    x = x_ref[0]
    kv = jnp.dot(x.astype(BF16), wkv_ref[...], preferred_element_type=F32)
    k, v = kv[:, :D_MODEL], kv[:, D_MODEL:]
    k_ref[0] = k
    v_ref[0] = v
    kb_ref[0] = k.astype(BF16)
    vb_ref[0] = v.astype(BF16)
    h = (x * (1.0 + sc_ref[0]) + sh_ref[0]).astype(BF16)
    q_ref[0] = (jnp.dot(h, wq_ref[...], preferred_element_type=F32) * ATTN_SCALE).astype(q_ref.dtype)


def _kvq(x, mod, mod_rows, row_off, wkv_bf, wq_bf, q_dtype):
    grp, rows, d = x.shape
    tile_m = min(rows, MLP_TILE_M)
    xspec = pl.BlockSpec((1, tile_m, d), lambda b, t: (b, t, 0))
    sds = lambda dt: jax.ShapeDtypeStruct((grp, rows, d), dt)
    return pl.pallas_call(
        _kvq_kernel,
        grid=(grp, rows // tile_m),
        in_specs=[xspec, _mod_spec(mod_rows, 0, row_off), _mod_spec(mod_rows, 1, row_off),
                  _resident((d, 2 * d), lambda b, t: (0, 0)), _resident((d, d), lambda b, t: (0, 0))],
        out_specs=[xspec] * 5,
        out_shape=[sds(F32), sds(F32), sds(BF16), sds(BF16), sds(q_dtype)],
        compiler_params=_params("parallel", "parallel"),
        name="kvq",
    )(x, mod, mod, wkv_bf, wq_bf)


def _softplus(z):
    return jnp.maximum(z, 0.0) + jnp.log(1.0 + jnp.exp(-jnp.abs(z)))


def _split_bf16(x):
    hi = x.astype(BF16)
    return hi, (x - hi.astype(F32)).astype(BF16)


def _triangle(n, copies):
    r = lax.broadcasted_iota(jnp.int32, (copies * n, n), 0)
    s = lax.broadcasted_iota(jnp.int32, (copies * n, n), 1)
    j = r
    for c in range(1, copies):
        j = jnp.where(r >= c * n, r - c * n, j)
    return jnp.where(j >= s, 1.0, 0.0).astype(BF16)


def _attn_kernel(blk, bias_ref, q_ref, k_ref, v_ref, o_ref):
    group = pl.program_id(1)
    i = pl.program_id(2)
    lane = lax.broadcasted_iota(jnp.int32, (1, SLAB), 1)
    row = lax.broadcasted_iota(jnp.int32, (blk, blk), 0)
    col = lax.broadcasted_iota(jnp.int32, (blk, blk), 1)
    causal = col < row
    tri = _triangle(blk, 2)
    heads = [(sl, hh) for sl in range(ATTN_SLABS_PER_STEP) for hh in range(HEADS_PER_SLAB)]
    qms, biases = [], []
    for sl, hh in heads:
        q = q_ref[0, :, sl * SLAB:(sl + 1) * SLAB]
        in_head = (lane >= hh * HEAD_DIM) & (lane < (hh + 1) * HEAD_DIM)
        qms.append(jnp.where(in_head, q, jnp.zeros_like(q)))
        biases.append(bias_ref[(group * ATTN_SLABS_PER_STEP + sl) * HEADS_PER_SLAB + hh])

    def tile(j, carry, mask):
        start = pl.multiple_of(j * blk, blk)
        kbs = [k_ref[0, pl.ds(start, blk), sl * SLAB:(sl + 1) * SLAB] for sl in range(ATTN_SLABS_PER_STEP)]
        vbs = [v_ref[0, pl.ds(start, blk), sl * SLAB:(sl + 1) * SLAB] for sl in range(ATTN_SLABS_PER_STEP)]
        zs = [lax.dot_general(qms[c], kbs[sl], (((1,), (1,)), ((), ())), preferred_element_type=F32) + biases[c]
              for c, (sl, _) in enumerate(heads)]
        nks = [_softplus(z) for z in zs]
        if mask is not None:
            nks = [jnp.where(mask, nk, 0.0) for nk in nks]
        incls = [jnp.dot(jnp.concatenate(_split_bf16(nk), axis=1), tri, preferred_element_type=F32)
                 for nk in nks]
        ws = [jnp.exp(z - incl - carry[c][0]) for c, (z, incl) in enumerate(zip(zs, incls))]
        if mask is not None:
            ws = [jnp.where(mask, a, 0.0) for a in ws]
        return tuple((carry[c][0] + incls[c][:, 0:1],
                      carry[c][1] + jnp.dot(ws[c].astype(BF16), vbs[sl], preferred_element_type=F32))
                     for c, (sl, _) in enumerate(heads))

    init = tuple((jnp.zeros((blk, 1), F32), jnp.zeros((blk, SLAB), F32)) for _ in heads)
    carry = tile(i, init, causal)
    carry = lax.fori_loop(0, i, lambda n, c: tile(i - 1 - n, c, None), carry)
    for sl in range(ATTN_SLABS_PER_STEP):
        o_ref[0, :, sl * SLAB:(sl + 1) * SLAB] = jnp.where(
            lane < HEAD_DIM, carry[2 * sl][1], carry[2 * sl + 1][1]).astype(o_ref.dtype)


def _attn_prompt(q_bf, k_bf, v_bf, bias):
    bsz, seq, d = q_bf.shape
    blk = min(seq, ATTN_BLOCK)
    width = ATTN_SLABS_PER_STEP * SLAB
    qspec = pl.BlockSpec((1, blk, width), lambda b, p, i: (b, i, p))
    kvspec = pl.BlockSpec((1, seq, width), lambda b, p, i: (b, 0, p))
    return pl.pallas_call(
        functools.partial(_attn_kernel, blk),
        grid=(bsz, N_SLABS // ATTN_SLABS_PER_STEP, seq // blk),
        in_specs=[pl.BlockSpec(memory_space=pltpu.SMEM), qspec, kvspec, kvspec],
        out_specs=qspec,
        out_shape=jax.ShapeDtypeStruct((bsz, seq, d), BF16),
        compiler_params=_params("parallel", "parallel", "arbitrary"),
        name="attn_prompt",
    )(bias, q_bf, k_bf, v_bf)


def _decode_kernel(n_pages_step, *refs):
    pt_ref = refs[0]
    q_ref, bias_ref, qpos_ref, kn_ref, vn_ref = refs[1:6]
    k_refs = refs[6:6 + n_pages_step]
    v_refs = refs[6 + n_pages_step:6 + 2 * n_pages_step]
    o_ref, r_ref, acc_ref = refs[6 + 2 * n_pages_step:]
    del pt_ref
    step = pl.program_id(1)
    t_new = q_ref.shape[1]
    rows = N_SLABS * HEADS_PER_SLAB * t_new
    slab_rows = HEADS_PER_SLAB * t_new

    q = q_ref[0]
    lane = lax.broadcasted_iota(jnp.int32, (1, SLAB), 1)
    qms = []
    for p in range(N_SLABS):
        qp = q[:, p * SLAB:(p + 1) * SLAB]
        parts = [jnp.where((lane >= hh * HEAD_DIM) & (lane < (hh + 1) * HEAD_DIM), qp, 0.0)
                 for hh in range(HEADS_PER_SLAB)]
        qms.append(jnp.concatenate(parts, axis=0).astype(BF16))
    bias = bias_ref[...]
    tri = _triangle(PAGE_SIZE, 1)

    def pages(kts, vts, mask):
        zs = [jnp.concatenate(
            [jnp.dot(qms[p], kt[0, p * SLAB:(p + 1) * SLAB, :].astype(BF16), preferred_element_type=F32)
             for p in range(N_SLABS)], axis=0) + bias for kt in kts]
        nks = [_softplus(z) for z in zs]
        if mask is not None:
            nks = [jnp.where(mask, nk, 0.0) for nk in nks]
        incls = []
        for nk in nks:
            both = jnp.dot(jnp.concatenate(_split_bf16(nk), axis=0), tri, preferred_element_type=F32)
            incls.append(both[:rows] + both[rows:])
        r = r_ref[...]
        ws = []
        for z, incl in zip(zs, incls):
            a = jnp.exp(z - incl - r)
            if mask is not None:
                a = jnp.where(mask, a, 0.0)
            ws.append(a.astype(BF16))
            r = r + incl[:, 0:1]
        r_ref[...] = r
        for p in range(N_SLABS):
            part = acc_ref[p * slab_rows:(p + 1) * slab_rows, :]
            for w, vt in zip(ws, vts):
                part = part + lax.dot_general(
                    w[p * slab_rows:(p + 1) * slab_rows, :], vt[0, p * SLAB:(p + 1) * SLAB, :].astype(BF16),
                    (((1,), (1,)), ((), ())), preferred_element_type=F32)
            acc_ref[p * slab_rows:(p + 1) * slab_rows, :] = part

    @pl.when(step == 0)
    def _():
        r_ref[...] = jnp.zeros_like(r_ref)
        acc_ref[...] = jnp.zeros_like(acc_ref)
        key = lax.broadcasted_iota(jnp.int32, (rows, PAGE_SIZE), 1)
        pages([kn_ref], [vn_ref], key < qpos_ref[...])

    order = list(reversed(range(n_pages_step)))
    pages([k_refs[g] for g in order], [v_refs[g] for g in order], None)

    @pl.when(step == pl.num_programs(1) - 1)
    def _():
        for p in range(N_SLABS):
            blk = acc_ref[p * slab_rows:(p + 1) * slab_rows, :]
            o_ref[0, :, p * SLAB:(p + 1) * SLAB] = jnp.where(
                lane < HEAD_DIM, blk[:t_new], blk[t_new:]).astype(o_ref.dtype)


def _attn_decode(q, k_new_pad, v_new_pad, cache_k, cache_v, page_table, bias_rows, qpos_rows):
    bsz, t_new, d = q.shape
    n_pages = page_table.shape[1]
    g = DECODE_PAGES_PER_STEP
    while n_pages % g:
        g //= 2
    n_steps = n_pages // g
    rows = N_HEADS * t_new

    def page_spec(slot):
        return pl.BlockSpec((1, d, PAGE_SIZE), lambda b, s, pt: (pt[b, (n_steps - 1 - s) * g + slot], 0, 0))

    per_b = lambda shape: pl.BlockSpec(shape, lambda b, s, pt: (b, 0, 0))
    grid_spec = pltpu.PrefetchScalarGridSpec(
        num_scalar_prefetch=1,
        grid=(bsz, n_steps),
        in_specs=[per_b((1, t_new, d)), pl.BlockSpec((rows, 1), lambda b, s, pt: (0, 0)),
                  pl.BlockSpec((rows, 1), lambda b, s, pt: (0, 0)),
                  per_b((1, d, PAGE_SIZE)), per_b((1, d, PAGE_SIZE))]
                 + [page_spec(slot) for slot in range(g)] * 2,
        out_specs=per_b((1, t_new, d)),
        scratch_shapes=[pltpu.VMEM((rows, 1), F32), pltpu.VMEM((rows, SLAB), F32)],
    )
    return pl.pallas_call(
        functools.partial(_decode_kernel, g),
        grid_spec=grid_spec,
        out_shape=jax.ShapeDtypeStruct((bsz, t_new, d), BF16),
        compiler_params=_params("parallel", "arbitrary"),
        name="attn_decode",
    )(page_table, q, bias_rows, qpos_rows, k_new_pad, v_new_pad, *([cache_k] * g), *([cache_v] * g))


def kernel(x_prompt, x_sample, cache_k, cache_v, state_pool, page_table, c_prompt, c_sample,
           w_ada, b_ada, ln_g, ln_b, w_pool, pool_scale, w_up, w_down, w_kv, w_q, w_o, b_break):
    bsz, seq, d = x_prompt.shape
    dec_b, dec_t, _ = x_sample.shape
    assert d == D_MODEL and w_ada.shape[0] == DEPTH == 2 and w_pool.shape[0] == 1 and w_q.shape[0] == 1
    past_len = page_table.shape[1] * PAGE_SIZE

    wu_bf, wd_bf = w_up.astype(BF16), w_down.astype(BF16)
    wkv_bf, wq_bf, wo_bf = w_kv.astype(BF16), w_q[0].astype(BF16), w_o[0].astype(BF16)
    wp_bf = w_pool[0].astype(BF16)
    ln_g4 = ln_g.reshape(DEPTH * 2, 1, d)
    ln_b4 = ln_b.reshape(DEPTH * 2, 1, d)

    ada = _ada(jnp.concatenate([c_prompt, c_sample], axis=0), w_ada, b_ada)
    mod_p = [ada[l].reshape(bsz + dec_b, 1, 6 * d) for l in range(DEPTH)]
    mod_s = [jnp.repeat(ada[l, bsz:], dec_t, axis=0)[None] for l in range(DEPTH)]
    xs_flat = lambda a: a.reshape(1, dec_b * dec_t, d)

    zero_hist = jnp.zeros((bsz, HIST_ROWS, d), F32)
    samp_hist = jnp.pad(state_pool[0], ((0, 0), (1, 0), (0, 0)))
    x1_p, hist_p = _mixer(x_prompt, zero_hist, mod_p[0], 0, 0, wp_bf, pool_scale, ln_g4, ln_b4, 0)
    x1_s, hist_s = _mixer(x_sample, samp_hist, mod_p[0], bsz, past_len, wp_bf, pool_scale, ln_g4, ln_b4, 0)
    x2_p = _mlp(x1_p, mod_p[0], 1, 0, wu_bf[0], wd_bf[0], ln_g4, ln_b4, 1)
    x2_s = _mlp(xs_flat(x1_s), mod_s[0], dec_b * dec_t, 0, wu_bf[0], wd_bf[0], ln_g4, ln_b4, 1)

    k_p, v_p, kb_p, vb_p, q_p = _kvq(x2_p, mod_p[1], 1, 0, wkv_bf, wq_bf, BF16)
    k_s, v_s, _, _, q_s = _kvq(x2_s, mod_s[1], dec_b * dec_t, 0, wkv_bf, wq_bf, F32)

    bias = b_break[0].astype(F32)
    o_p = _attn_prompt(q_p, kb_p, vb_p, bias)
    pages = lambda c: jnp.transpose(c, (0, 2, 3, 1)).reshape(-1, d, PAGE_SIZE)
    pad_new = lambda a: jnp.pad(jnp.transpose(a.reshape(dec_b, dec_t, d), (0, 2, 1)),
                                ((0, 0), (0, 0), (0, PAGE_SIZE - dec_t)))
    o_s = _attn_decode(q_s.reshape(dec_b, dec_t, d), pad_new(k_s), pad_new(v_s), pages(cache_k), pages(cache_v),
                       page_table, jnp.repeat(bias, dec_t)[:, None],
                       jnp.tile(jnp.arange(dec_t, dtype=jnp.int32), N_HEADS)[:, None])

    y_p = _mlp(x2_p, mod_p[1], 1, 0, wu_bf[1], wd_bf[1], ln_g4, ln_b4, 3, oproj=(o_p, wo_bf, 2))
    y_s = _mlp(x2_s, mod_s[1], dec_b * dec_t, 0, wu_bf[1], wd_bf[1], ln_g4, ln_b4, 3,
               oproj=(xs_flat(o_s), wo_bf, 2))

    heads = lambda a, b, t: a.reshape(b, t, N_HEADS, HEAD_DIM)
    return (y_p, y_s.reshape(dec_b, dec_t, d),
            heads(k_p, bsz, seq), heads(v_p, bsz, seq), hist_p[None, :, 1:],
            heads(k_s, dec_b, dec_t), heads(v_s, dec_b, dec_t), hist_s[None, :, 1:])
```

```python
import functools

import jax
import jax.numpy as jnp
from jax import lax
from jax.experimental import pallas as pl
from jax.experimental.pallas import tpu as pltpu

F32 = jnp.float32
BF16 = jnp.bfloat16

D_MODEL = 1024
DEPTH = 2
N_HEADS = 16
HEAD_DIM = 64
HEADS_PER_SLAB = 2
SLAB = HEADS_PER_SLAB * HEAD_DIM
N_SLABS = N_HEADS // HEADS_PER_SLAB
D_FF = 4 * D_MODEL
POOL_WINDOWS = (2, 4, 8, 16)
POOL_GROUP_DIM = D_MODEL // len(POOL_WINDOWS)
POOL_HIST = max(POOL_WINDOWS) - 1
HIST_ROWS = 2 * max(POOL_WINDOWS)
MIXER_SEQS_PER_STEP = 8
LN_EPS = 1e-5
DN_ALPHA = float((2 * DEPTH) ** 0.25)
ATTN_SCALE = HEAD_DIM ** -0.5
PAGE_SIZE = 128

VMEM_LIMIT_BYTES = 56 * 1024 * 1024
ADA_TILE_N = 1536
MLP_TILE_M = 512
MLP_TILE_F = 512
ATTN_BLOCK = 256
ATTN_SLABS_PER_STEP = 4
DECODE_PAGES_PER_STEP = 16


def _params(*sem):
    return pltpu.CompilerParams(dimension_semantics=sem, vmem_limit_bytes=VMEM_LIMIT_BYTES)


def _layer_norm(y, g, b):
    mu = jnp.mean(y, axis=-1, keepdims=True)
    yc = y - mu
    var = jnp.mean(yc * yc, axis=-1, keepdims=True)
    return yc * lax.rsqrt(var + LN_EPS) * g + b


def _resident(shape, index_map):
    return pl.BlockSpec(shape, index_map, pipeline_mode=pl.Buffered(1))


def _ada_kernel(c_ref, w_ref, b_ref, o_ref):
    c = c_ref[...]
    s = c * jax.nn.sigmoid(c)
    o_ref[0] = jnp.dot(s.astype(BF16), w_ref[0].astype(BF16), preferred_element_type=F32) + b_ref[0]


def _ada(c_all, w_ada, b_ada):
    n_layers, d, n = w_ada.shape
    m = c_all.shape[0]
    return pl.pallas_call(
        _ada_kernel,
        grid=(n_layers, n // ADA_TILE_N),
        in_specs=[
            pl.BlockSpec((m, d), lambda l, j: (0, 0)),
            pl.BlockSpec((1, d, ADA_TILE_N), lambda l, j: (l, 0, j)),
            pl.BlockSpec((1, 1, ADA_TILE_N), lambda l, j: (l, 0, j)),
        ],
        out_specs=pl.BlockSpec((1, m, ADA_TILE_N), lambda l, j: (l, 0, j)),
        out_shape=jax.ShapeDtypeStruct((n_layers, m, n), F32),
        compiler_params=_params("parallel", "parallel"),
        name="ada",
    )(c_all, w_ada, b_ada.reshape(n_layers, 1, n))


def _mod_spec(rows, chunk, row_off):
    return pl.BlockSpec((1, rows, D_MODEL), lambda b, t: (b + row_off, 0, chunk))


def _mixer_kernel(pos0, tile_t, n_seq, x_ref, hist_ref, sh_ref, sc_ref, g_ref, wp_ref, ps_ref,
                  lng_ref, lnb_ref, o_ref, ho_ref, ext_ref, s2_ref, s4_ref, s8_ref):
    t = pl.program_id(1)
    gd = POOL_GROUP_DIM
    rows_seq = HIST_ROWS + tile_t
    n = n_seq * rows_seq

    @pl.when(t == 0)
    def _():
        for s in range(n_seq):
            ext_ref[s * rows_seq:s * rows_seq + HIST_ROWS, :] = hist_ref[s]

    for s in range(n_seq):
        ext_ref[s * rows_seq + HIST_ROWS:(s + 1) * rows_seq, :] = x_ref[s] * (1.0 + sc_ref[s]) + sh_ref[s]
    s2_ref[8:n, :] = ext_ref[8:n, gd:] + ext_ref[7:n - 1, gd:]
    s4_ref[16:n, :] = s2_ref[16:n, :] + s2_ref[14:n - 2, :]
    s8_ref[24:n, :] = s4_ref[24:n, gd:] + s4_ref[20:n - 4, gd:]

    pos = pos0 + t * tile_t + lax.broadcasted_iota(jnp.int32, (tile_t, 1), 0)
    inv_cnt = [1.0 / jnp.minimum(w, pos + 1).astype(F32) for w in POOL_WINDOWS]
    ds = [[] for _ in POOL_WINDOWS]
    for s in range(n_seq):
        a, b = s * rows_seq + HIST_ROWS, (s + 1) * rows_seq
        wsums = [ext_ref[a:b, :gd] + ext_ref[a - 1:b - 1, :gd], s4_ref[a:b, :gd], s8_ref[a:b, :gd],
                 s8_ref[a:b, gd:] + s8_ref[a - 8:b - 8, gd:]]
        for g, wsum in enumerate(wsums):
            ds[g].append(wsum * inv_cnt[g] - ext_ref[a:b, g * gd:(g + 1) * gd])
    mix = jnp.concatenate(
        [jnp.dot(jnp.concatenate(ds[g], axis=0).astype(BF16), wp_ref[g], preferred_element_type=F32)
         for g in range(len(POOL_WINDOWS))], axis=-1) * ps_ref[...]
    for s in range(n_seq):
        y = DN_ALPHA * x_ref[s] + g_ref[s] * mix[s * tile_t:(s + 1) * tile_t]
        o_ref[s] = _layer_norm(y, lng_ref[0], lnb_ref[0])
        tail = ext_ref[(s + 1) * rows_seq - HIST_ROWS:(s + 1) * rows_seq, :]
        ho_ref[s] = tail
        ext_ref[s * rows_seq:s * rows_seq + HIST_ROWS, :] = tail


def _mixer(x, hist, mod, row_off, pos0, n_seq, wp_bf, pool_scale, ln_g, ln_b, ln_idx):
    bsz, seq, d = x.shape
    tile_t = min(seq, 256)
    assert bsz % n_seq == 0 and row_off % n_seq == 0 and tile_t % 8 == 0
    n = n_seq * (HIST_ROWS + tile_t)
    gd = POOL_GROUP_DIM
    mod_spec = lambda chunk: pl.BlockSpec((n_seq, 1, d), lambda b, t: (b + row_off // n_seq, 0, chunk))
    return pl.pallas_call(
        functools.partial(_mixer_kernel, pos0, tile_t, n_seq),
        grid=(bsz // n_seq, seq // tile_t),
        in_specs=[
            pl.BlockSpec((n_seq, tile_t, d), lambda b, t: (b, t, 0)),
            pl.BlockSpec((n_seq, HIST_ROWS, d), lambda b, t: (b, 0, 0)),
            mod_spec(0), mod_spec(1), mod_spec(2),
            pl.BlockSpec(wp_bf.shape, lambda b, t: (0, 0, 0)),
            pl.BlockSpec((1, d), lambda b, t: (0, 0)),
            pl.BlockSpec((1, 1, d), lambda b, t: (ln_idx, 0, 0)),
            pl.BlockSpec((1, 1, d), lambda b, t: (ln_idx, 0, 0)),
        ],
        out_specs=[
            pl.BlockSpec((n_seq, tile_t, d), lambda b, t: (b, t, 0)),
            pl.BlockSpec((n_seq, HIST_ROWS, d), lambda b, t: (b, 0, 0)),
        ],
        out_shape=[jax.ShapeDtypeStruct((bsz, seq, d), F32),
                   jax.ShapeDtypeStruct((bsz, HIST_ROWS, d), F32)],
        scratch_shapes=[pltpu.VMEM((n, d), F32), pltpu.VMEM((n, d - gd), F32),
                        pltpu.VMEM((n, d - gd), F32), pltpu.VMEM((n, d - 2 * gd), F32)],
        compiler_params=_params("parallel", "arbitrary"),
        name="mixer",
    )(x, hist, mod, mod, mod, wp_bf, pool_scale, ln_g, ln_b)


def _mlp_kernel(has_oproj, *refs):
    if has_oproj:
        (x_ref, o_ref, g1_ref, wo_ref, lng1_ref, lnb1_ref,
         sh_ref, sc_ref, g_ref, wu_ref, wd_ref, lng_ref, lnb_ref, out_ref, acc_ref) = refs
        mix = jnp.dot(o_ref[0], wo_ref[...], preferred_element_type=F32)
        x = _layer_norm(DN_ALPHA * x_ref[0] + g1_ref[0] * mix, lng1_ref[0], lnb1_ref[0])
    else:
        (x_ref, sh_ref, sc_ref, g_ref, wu_ref, wd_ref, lng_ref, lnb_ref, out_ref, acc_ref) = refs
        x = x_ref[0]
    h = (x * (1.0 + sc_ref[0]) + sh_ref[0]).astype(BF16)
    for c in range(D_FF // MLP_TILE_F):
        lo, hi = c * MLP_TILE_F, (c + 1) * MLP_TILE_F
        u = jnp.dot(h, wu_ref[:, lo:hi], preferred_element_type=F32)
        u = jnp.maximum(u, 0.0)
        contrib = jnp.dot((u * u).astype(BF16), wd_ref[lo:hi, :], preferred_element_type=F32)
        if c == 0:
            acc_ref[...] = contrib
        else:
            acc_ref[...] += contrib
    y = DN_ALPHA * x + g_ref[0] * acc_ref[...]
    out_ref[0] = _layer_norm(y, lng_ref[0], lnb_ref[0])


def _mlp(x, mod, mod_rows, row_off, wu_bf, wd_bf, ln_g, ln_b, ln_idx, oproj=None):
    grp, rows, d = x.shape
    tile_m = min(rows, MLP_TILE_M)
    xspec = pl.BlockSpec((1, tile_m, d), lambda b, t: (b, t, 0))
    lnspec = lambda idx: pl.BlockSpec((1, 1, d), lambda b, t: (idx, 0, 0))
    ins, specs = [x], [xspec]
    if oproj is not None:
        o_bf, wo_bf, ln_idx1 = oproj
        ins += [o_bf, mod, wo_bf, ln_g, ln_b]
        specs += [xspec, _mod_spec(mod_rows, 2, row_off), _resident((d, d), lambda b, t: (0, 0)),
                  lnspec(ln_idx1), lnspec(ln_idx1)]
    ins += [mod, mod, mod, wu_bf, wd_bf, ln_g, ln_b]
    specs += [_mod_spec(mod_rows, 3, row_off), _mod_spec(mod_rows, 4, row_off), _mod_spec(mod_rows, 5, row_off),
              _resident((d, D_FF), lambda b, t: (0, 0)), _resident((D_FF, d), lambda b, t: (0, 0)),
              lnspec(ln_idx), lnspec(ln_idx)]
    return pl.pallas_call(
        functools.partial(_mlp_kernel, oproj is not None),
        grid=(grp, rows // tile_m),
        in_specs=specs,
        out_specs=xspec,
        out_shape=jax.ShapeDtypeStruct((grp, rows, d), F32),
        scratch_shapes=[pltpu.VMEM((tile_m, d), F32)],
        compiler_params=_params("parallel", "parallel"),
        name="mlp_oproj" if oproj is not None else "mlp",
    )(*ins)


def _kvq_kernel(decode, x_ref, sh_ref, sc_ref, wkv_ref, wq_ref, k_ref, v_ref, k2_ref, v2_ref, q_ref):
    x = x_ref[0]
    kv = jnp.dot(x.astype(BF16), wkv_ref[...], preferred_element_type=F32)
    k, v = kv[:, :D_MODEL], kv[:, D_MODEL:]
    k_ref[0] = k
    v_ref[0] = v
    if decode:
        k2_ref[0] = k.T
        v2_ref[0] = v.T
    else:
        k2_ref[0] = k.astype(BF16)
        blk = v2_ref.shape[3]
        for kb in range(v2_ref.shape[1]):
            v2_ref[0, kb] = v[kb * blk:(kb + 1) * blk, :].T.astype(BF16)
    h = (x * (1.0 + sc_ref[0]) + sh_ref[0]).astype(BF16)
    q_ref[0] = (jnp.dot(h, wq_ref[...], preferred_element_type=F32) * ATTN_SCALE).astype(q_ref.dtype)


def _kvq(x, mod, mod_rows, row_off, wkv_bf, wq_bf, decode):
    grp, rows, d = x.shape
    tile_m = min(rows, MLP_TILE_M)
    xspec = pl.BlockSpec((1, tile_m, d), lambda b, t: (b, t, 0))
    sds = lambda dt: jax.ShapeDtypeStruct((grp, rows, d), dt)
    if decode:
        tspec = pl.BlockSpec((1, d, tile_m), lambda b, t: (b, 0, t))
        specs2, sds2 = [tspec, tspec], [jax.ShapeDtypeStruct((grp, d, rows), F32)] * 2
    else:
        blk = min(rows, ATTN_BLOCK)
        specs2 = [xspec, pl.BlockSpec((1, tile_m // blk, d, blk), lambda b, t: (b, t, 0, 0))]
        sds2 = [sds(BF16), jax.ShapeDtypeStruct((grp, rows // blk, d, blk), BF16)]
    return pl.pallas_call(
        functools.partial(_kvq_kernel, decode),
        grid=(grp, rows // tile_m),
        in_specs=[xspec, _mod_spec(mod_rows, 0, row_off), _mod_spec(mod_rows, 1, row_off),
                  _resident((d, 2 * d), lambda b, t: (0, 0)), _resident((d, d), lambda b, t: (0, 0))],
        out_specs=[xspec, xspec] + specs2 + [xspec],
        out_shape=[sds(F32), sds(F32)] + sds2 + [sds(F32 if decode else BF16)],
        compiler_params=_params("parallel", "parallel"),
        name="kvq",
    )(x, mod, mod, wkv_bf, wq_bf)


def _softplus(z):
    neg_abs = lax.bitcast_convert_type(lax.bitcast_convert_type(z, jnp.uint32) | jnp.uint32(0x80000000), F32)
    return jnp.maximum(z, 0.0) + jnp.log(1.0 + jnp.exp(neg_abs))


def _triangle(n, transposed=False):
    r = lax.broadcasted_iota(jnp.int32, (n, n), 0)
    c = lax.broadcasted_iota(jnp.int32, (n, n), 1)
    return jnp.where((c > r) if transposed else (r > c), 1.0, 0.0).astype(BF16)


def _attn_kernel(blk, bias_ref, q_ref, k_ref, vt_ref, o_ref):
    group = pl.program_id(1)
    i = pl.program_id(2)
    lane = lax.broadcasted_iota(jnp.int32, (1, SLAB), 1)
    key = lax.broadcasted_iota(jnp.int32, (blk, blk), 0)
    qry = lax.broadcasted_iota(jnp.int32, (blk, blk), 1)
    causal = key < qry
    tri = _triangle(blk, transposed=True)
    heads = [(sl, hh) for sl in range(ATTN_SLABS_PER_STEP) for hh in range(HEADS_PER_SLAB)]
    qms, biases = [], []
    for sl, hh in heads:
        q = q_ref[0, :, sl * SLAB:(sl + 1) * SLAB]
        in_head = (lane >= hh * HEAD_DIM) & (lane < (hh + 1) * HEAD_DIM)
        qms.append(jnp.where(in_head, q, jnp.zeros_like(q)))
        biases.append(bias_ref[(group * ATTN_SLABS_PER_STEP + sl) * HEADS_PER_SLAB + hh])

    def tile(j, carry, mask):
        start = pl.multiple_of(j * blk, blk)
        kbs = [k_ref[0, pl.ds(start, blk), sl * SLAB:(sl + 1) * SLAB] for sl in range(ATTN_SLABS_PER_STEP)]
        zs = [lax.dot_general(kbs[sl], qms[c], (((1,), (1,)), ((), ())), preferred_element_type=F32) + biases[c]
              for c, (sl, _) in enumerate(heads)]
        nks = [_softplus(z) for z in zs]
        if mask is not None:
            nks = [jnp.where(mask, nk, 0.0) for nk in nks]
        excls = [jnp.dot(tri, nk.astype(BF16), preferred_element_type=F32) for nk in nks]
        ws = [jnp.exp((zs[c] - nks[c]) - excls[c] - carry[c][0]) for c in range(len(heads))]
        if mask is not None:
            ws = [jnp.where(mask, a, 0.0) for a in ws]
        new = []
        for c, (sl, hh) in enumerate(heads):
            lo = sl * SLAB + hh * HEAD_DIM
            vt = vt_ref[0, j, lo:lo + HEAD_DIM, :]
            new.append((carry[c][0] + (excls[c][0:1, :] + nks[c][0:1, :]),
                        carry[c][1] + jnp.dot(vt, ws[c].astype(BF16), preferred_element_type=F32)))
        return tuple(new)

    init = tuple((jnp.zeros((1, blk), F32), jnp.zeros((HEAD_DIM, blk), F32)) for _ in heads)
    carry = tile(i, init, causal)
    carry = lax.fori_loop(0, i, lambda n, c: tile(i - 1 - n, c, None), carry)
    for sl in range(ATTN_SLABS_PER_STEP):
        ot = jnp.concatenate([carry[HEADS_PER_SLAB * sl + hh][1] for hh in range(HEADS_PER_SLAB)], axis=0)
        o_ref[0, :, sl * SLAB:(sl + 1) * SLAB] = ot.T.astype(o_ref.dtype)


def _attn_prompt(q_bf, k_bf, vt_bf, bias):
    bsz, seq, d = q_bf.shape
    blk = vt_bf.shape[3]
    width = ATTN_SLABS_PER_STEP * SLAB
    qspec = pl.BlockSpec((1, blk, width), lambda b, p, i: (b, i, p))
    kspec = pl.BlockSpec((1, seq, width), lambda b, p, i: (b, 0, p))
    vspec = pl.BlockSpec((1, seq // blk, width, blk), lambda b, p, i: (b, 0, p, 0))
    return pl.pallas_call(
        functools.partial(_attn_kernel, blk),
        grid=(bsz, N_SLABS // ATTN_SLABS_PER_STEP, seq // blk),
        in_specs=[pl.BlockSpec(memory_space=pltpu.SMEM), qspec, kspec, vspec],
        out_specs=qspec,
        out_shape=jax.ShapeDtypeStruct((bsz, seq, d), BF16),
        compiler_params=_params("parallel", "parallel", "arbitrary"),
        name="attn_prompt",
    )(bias, q_bf, k_bf, vt_bf)


def _decode_kernel(n_pages_step, *refs):
    pt_ref = refs[0]
    q_ref, bias_ref, qpos_ref, kn_ref, vn_ref = refs[1:6]
    k_refs = refs[6:6 + n_pages_step]
    v_refs = refs[6 + n_pages_step:6 + 2 * n_pages_step]
    o_ref, r_ref, acc_ref = refs[6 + 2 * n_pages_step:]
    del pt_ref
    step = pl.program_id(1)
    t_new = q_ref.shape[1]
    rows = N_SLABS * HEADS_PER_SLAB * t_new
    slab_rows = HEADS_PER_SLAB * t_new

    q = q_ref[0]
    lane = lax.broadcasted_iota(jnp.int32, (1, SLAB), 1)
    qms = []
    for p in range(N_SLABS):
        qp = q[:, p * SLAB:(p + 1) * SLAB]
        parts = [jnp.where((lane >= hh * HEAD_DIM) & (lane < (hh + 1) * HEAD_DIM), qp, 0.0)
                 for hh in range(HEADS_PER_SLAB)]
        qms.append(jnp.concatenate(parts, axis=0).astype(BF16))
    bias = bias_ref[...]
    tri = _triangle(PAGE_SIZE)

    def pages(kts, vts, mask):
        zs = [jnp.concatenate(
            [jnp.dot(qms[p], kt[0, p * SLAB:(p + 1) * SLAB, :].astype(BF16), preferred_element_type=F32)
             for p in range(N_SLABS)], axis=0) + bias for kt in kts]
        nks = [_softplus(z) for z in zs]
        if mask is not None:
            nks = [jnp.where(mask, nk, 0.0) for nk in nks]
        excls = [jnp.dot(nk.astype(BF16), tri, preferred_element_type=F32) for nk in nks]
        r = r_ref[...]
        ws = []
        for z, nk, excl in zip(zs, nks, excls):
            a = jnp.exp((z - nk) - excl - r)
            if mask is not None:
                a = jnp.where(mask, a, 0.0)
            ws.append(a.astype(BF16))
            r = r + (excl[:, 0:1] + nk[:, 0:1])
        r_ref[...] = r
        for p in range(N_SLABS):
            part = acc_ref[p * slab_rows:(p + 1) * slab_rows, :]
            for w, vt in zip(ws, vts):
                part = part + lax.dot_general(
                    w[p * slab_rows:(p + 1) * slab_rows, :], vt[0, p * SLAB:(p + 1) * SLAB, :].astype(BF16),
                    (((1,), (1,)), ((), ())), preferred_element_type=F32)
            acc_ref[p * slab_rows:(p + 1) * slab_rows, :] = part

    @pl.when(step == 0)
    def _():
        r_ref[...] = jnp.zeros_like(r_ref)
        acc_ref[...] = jnp.zeros_like(acc_ref)
        off = (pl.program_id(0) * t_new) % PAGE_SIZE
        key = lax.broadcasted_iota(jnp.int32, (rows, PAGE_SIZE), 1) - off
        pages([kn_ref], [vn_ref], (key >= 0) & (key < qpos_ref[...]))

    order = list(reversed(range(n_pages_step)))
    pages([k_refs[g] for g in order], [v_refs[g] for g in order], None)

    @pl.when(step == pl.num_programs(1) - 1)
    def _():
        for p in range(N_SLABS):
            blk = acc_ref[p * slab_rows:(p + 1) * slab_rows, :]
            o_ref[0, :, p * SLAB:(p + 1) * SLAB] = jnp.where(
                lane < HEAD_DIM, blk[:t_new], blk[t_new:]).astype(o_ref.dtype)


def _attn_decode(q, kt_new, vt_new, cache_k, cache_v, page_table, bias_rows, qpos_rows):
    bsz, t_new, d = q.shape
    assert PAGE_SIZE % t_new == 0 and kt_new.shape[2] % PAGE_SIZE == 0
    n_pages = page_table.shape[1]
    g = DECODE_PAGES_PER_STEP
    while n_pages % g:
        g //= 2
    n_steps = n_pages // g
    rows = N_HEADS * t_new

    def page_spec(slot):
        return pl.BlockSpec((1, d, PAGE_SIZE), lambda b, s, pt: (pt[b, (n_steps - 1 - s) * g + slot], 0, 0))

    per_b = lambda shape: pl.BlockSpec(shape, lambda b, s, pt: (b, 0, 0))
    new_spec = pl.BlockSpec((1, d, PAGE_SIZE), lambda b, s, pt: (0, 0, (b * t_new) // PAGE_SIZE))
    grid_spec = pltpu.PrefetchScalarGridSpec(
        num_scalar_prefetch=1,
        grid=(bsz, n_steps),
        in_specs=[per_b((1, t_new, d)), pl.BlockSpec((rows, 1), lambda b, s, pt: (0, 0)),
                  pl.BlockSpec((rows, 1), lambda b, s, pt: (0, 0)), new_spec, new_spec]
                 + [page_spec(slot) for slot in range(g)] * 2,
        out_specs=per_b((1, t_new, d)),
        scratch_shapes=[pltpu.VMEM((rows, 1), F32), pltpu.VMEM((rows, SLAB), F32)],
    )
    return pl.pallas_call(
        functools.partial(_decode_kernel, g),
        grid_spec=grid_spec,
        out_shape=jax.ShapeDtypeStruct((bsz, t_new, d), BF16),
        compiler_params=_params("parallel", "arbitrary"),
        name="attn_decode",
    )(page_table, q, bias_rows, qpos_rows, kt_new, vt_new, *([cache_k] * g), *([cache_v] * g))


def kernel(x_prompt, x_sample, cache_k, cache_v, state_pool, page_table, c_prompt, c_sample,
           w_ada, b_ada, ln_g, ln_b, w_pool, pool_scale, w_up, w_down, w_kv, w_q, w_o, b_break):
    bsz, seq, d = x_prompt.shape
    dec_b, dec_t, _ = x_sample.shape
    assert d == D_MODEL and w_ada.shape[0] == DEPTH == 2 and w_pool.shape[0] == 1 and w_q.shape[0] == 1
    past_len = page_table.shape[1] * PAGE_SIZE

    wu_bf, wd_bf = w_up.astype(BF16), w_down.astype(BF16)
    wkv_bf, wq_bf, wo_bf = w_kv.astype(BF16), w_q[0].astype(BF16), w_o[0].astype(BF16)
    wp_bf = w_pool[0].astype(BF16)
    ln_g4 = ln_g.reshape(DEPTH * 2, 1, d)
    ln_b4 = ln_b.reshape(DEPTH * 2, 1, d)

    ada = _ada(jnp.concatenate([c_prompt, c_sample], axis=0), w_ada, b_ada)
    mod_p = [ada[l].reshape(bsz + dec_b, 1, 6 * d) for l in range(DEPTH)]
    mod_s = [jnp.repeat(ada[l, bsz:], dec_t, axis=0)[None] for l in range(DEPTH)]
    xs_flat = lambda a: a.reshape(1, dec_b * dec_t, d)

    zero_hist = jnp.zeros((bsz, HIST_ROWS, d), F32)
    samp_hist = jnp.pad(state_pool[0], ((0, 0), (HIST_ROWS - POOL_HIST, 0), (0, 0)))
    n_seq = MIXER_SEQS_PER_STEP
    while dec_b % n_seq or bsz % n_seq:
        n_seq //= 2
    x1_p, hist_p = _mixer(x_prompt, zero_hist, mod_p[0], 0, 0, 1, wp_bf, pool_scale, ln_g4, ln_b4, 0)
    x1_s, hist_s = _mixer(x_sample, samp_hist, mod_p[0], bsz, past_len, n_seq, wp_bf, pool_scale,
                          ln_g4, ln_b4, 0)
    x2_p = _mlp(x1_p, mod_p[0], 1, 0, wu_bf[0], wd_bf[0], ln_g4, ln_b4, 1)
    x2_s = _mlp(xs_flat(x1_s), mod_s[0], dec_b * dec_t, 0, wu_bf[0], wd_bf[0], ln_g4, ln_b4, 1)

    k_p, v_p, kb_p, vtb_p, q_p = _kvq(x2_p, mod_p[1], 1, 0, wkv_bf, wq_bf, decode=False)
    k_s, v_s, kt_s, vt_s, q_s = _kvq(x2_s, mod_s[1], dec_b * dec_t, 0, wkv_bf, wq_bf, decode=True)

    bias = b_break[0].astype(F32)
    o_p = _attn_prompt(q_p, kb_p, vtb_p, bias)
    pages = lambda c: jnp.transpose(c, (0, 2, 3, 1)).reshape(-1, d, PAGE_SIZE)
    pad_cols = (-dec_b * dec_t) % PAGE_SIZE
    if pad_cols:
        kt_s, vt_s = (jnp.pad(a, ((0, 0), (0, 0), (0, pad_cols))) for a in (kt_s, vt_s))
    o_s = _attn_decode(q_s.reshape(dec_b, dec_t, d), kt_s, vt_s, pages(cache_k), pages(cache_v),
                       page_table, jnp.repeat(bias, dec_t)[:, None],
                       jnp.tile(jnp.arange(dec_t, dtype=jnp.int32), N_HEADS)[:, None])

    y_p = _mlp(x2_p, mod_p[1], 1, 0, wu_bf[1], wd_bf[1], ln_g4, ln_b4, 3, oproj=(o_p, wo_bf, 2))
    y_s = _mlp(x2_s, mod_s[1], dec_b * dec_t, 0, wu_bf[1], wd_bf[1], ln_g4, ln_b4, 3,
               oproj=(xs_flat(o_s), wo_bf, 2))

    heads = lambda a, b, t: a.reshape(b, t, N_HEADS, HEAD_DIM)
    return (y_p, y_s.reshape(dec_b, dec_t, d),
            heads(k_p, bsz, seq), heads(v_p, bsz, seq), hist_p[None, :, HIST_ROWS - POOL_HIST:],
            heads(k_s, dec_b, dec_t), heads(v_s, dec_b, dec_t), hist_s[None, :, HIST_ROWS - POOL_HIST:])
```

```python
import functools

import jax
import jax.numpy as jnp
from jax import lax
from jax.experimental import pallas as pl
from jax.experimental.pallas import tpu as pltpu

F32 = jnp.float32
BF16 = jnp.bfloat16

D_MODEL = 1024
DEPTH = 2
N_HEADS = 16
HEAD_DIM = 64
HEADS_PER_SLAB = 2
SLAB = HEADS_PER_SLAB * HEAD_DIM
N_SLABS = N_HEADS // HEADS_PER_SLAB
D_FF = 4 * D_MODEL
POOL_WINDOWS = (2, 4, 8, 16)
POOL_GROUP_DIM = D_MODEL // len(POOL_WINDOWS)
POOL_HIST = max(POOL_WINDOWS) - 1
HIST_ROWS = 2 * max(POOL_WINDOWS)
MIXER_SEQS_PER_STEP = 8
LN_EPS = 1e-5
DN_ALPHA = float((2 * DEPTH) ** 0.25)
ATTN_SCALE = HEAD_DIM ** -0.5
PAGE_SIZE = 128

VMEM_LIMIT_BYTES = 56 * 1024 * 1024
ADA_TILE_N = 1536
MLP_TILE_M = 512
MLP_TILE_F = 512
ATTN_BLOCK = 256
ATTN_SLABS_PER_STEP = 4
ATTN_HEADS_PER_WAVE = 8
DECODE_PAGES_PER_STEP = 16


def _params(*sem):
    return pltpu.CompilerParams(dimension_semantics=sem, vmem_limit_bytes=VMEM_LIMIT_BYTES)


def _layer_norm(y, g, b):
    mu = jnp.mean(y, axis=-1, keepdims=True)
    yc = y - mu
    var = jnp.mean(yc * yc, axis=-1, keepdims=True)
    return yc * lax.rsqrt(var + LN_EPS) * g + b


def _resident(shape, index_map):
    return pl.BlockSpec(shape, index_map, pipeline_mode=pl.Buffered(1))


def _ada_kernel(c_ref, w_ref, b_ref, o_ref):
    c = c_ref[...]
    s = c * jax.nn.sigmoid(c)
    o_ref[0] = jnp.dot(s.astype(BF16), w_ref[0].astype(BF16), preferred_element_type=F32) + b_ref[0]


def _ada(c_all, w_ada, b_ada):
    n_layers, d, n = w_ada.shape
    m = c_all.shape[0]
    return pl.pallas_call(
        _ada_kernel,
        grid=(n_layers, n // ADA_TILE_N),
        in_specs=[
            pl.BlockSpec((m, d), lambda l, j: (0, 0)),
            pl.BlockSpec((1, d, ADA_TILE_N), lambda l, j: (l, 0, j)),
            pl.BlockSpec((1, 1, ADA_TILE_N), lambda l, j: (l, 0, j)),
        ],
        out_specs=pl.BlockSpec((1, m, ADA_TILE_N), lambda l, j: (l, 0, j)),
        out_shape=jax.ShapeDtypeStruct((n_layers, m, n), F32),
        compiler_params=_params("parallel", "parallel"),
        name="ada",
    )(c_all, w_ada, b_ada.reshape(n_layers, 1, n))


def _mod_spec(rows, chunk, row_off):
    return pl.BlockSpec((1, rows, D_MODEL), lambda b, t: (b + row_off, 0, chunk))


def _mixer_kernel(pos0, tile_t, n_seq, x_ref, hist_ref, sh_ref, sc_ref, g_ref, wp_ref, ps_ref,
                  lng_ref, lnb_ref, o_ref, ho_ref, ext_ref, s2_ref, s4_ref, s8_ref):
    t = pl.program_id(1)
    gd = POOL_GROUP_DIM
    rows_seq = HIST_ROWS + tile_t
    n = n_seq * rows_seq

    @pl.when(t == 0)
    def _():
        for s in range(n_seq):
            ext_ref[s * rows_seq:s * rows_seq + HIST_ROWS - POOL_HIST, :] = jnp.zeros(
                (HIST_ROWS - POOL_HIST, D_MODEL), F32)
            ext_ref[s * rows_seq + HIST_ROWS - POOL_HIST:s * rows_seq + HIST_ROWS, :] = hist_ref[s]

    for s in range(n_seq):
        ext_ref[s * rows_seq + HIST_ROWS:(s + 1) * rows_seq, :] = x_ref[s] * (1.0 + sc_ref[s]) + sh_ref[s]
    s2_ref[8:n, :] = ext_ref[8:n, gd:] + ext_ref[7:n - 1, gd:]
    s4_ref[16:n, :] = s2_ref[16:n, :] + s2_ref[14:n - 2, :]
    s8_ref[24:n, :] = s4_ref[24:n, gd:] + s4_ref[20:n - 4, gd:]

    pos = pos0 + t * tile_t + lax.broadcasted_iota(jnp.int32, (tile_t, 1), 0)
    inv_cnt = [1.0 / jnp.minimum(w, pos + 1).astype(F32) for w in POOL_WINDOWS]
    ds = [[] for _ in POOL_WINDOWS]
    for s in range(n_seq):
        a, b = s * rows_seq + HIST_ROWS, (s + 1) * rows_seq
        wsums = [ext_ref[a:b, :gd] + ext_ref[a - 1:b - 1, :gd], s4_ref[a:b, :gd], s8_ref[a:b, :gd],
                 s8_ref[a:b, gd:] + s8_ref[a - 8:b - 8, gd:]]
        for g, wsum in enumerate(wsums):
            ds[g].append(wsum * inv_cnt[g] - ext_ref[a:b, g * gd:(g + 1) * gd])
    mix = jnp.concatenate(
        [jnp.dot(jnp.concatenate(ds[g], axis=0).astype(BF16), wp_ref[g], preferred_element_type=F32)
         for g in range(len(POOL_WINDOWS))], axis=-1) * ps_ref[...]
    for s in range(n_seq):
        y = DN_ALPHA * x_ref[s] + g_ref[s] * mix[s * tile_t:(s + 1) * tile_t]
        o_ref[s] = _layer_norm(y, lng_ref[0], lnb_ref[0])
        ho_ref[s] = ext_ref[(s + 1) * rows_seq - POOL_HIST:(s + 1) * rows_seq, :]
        ext_ref[s * rows_seq:s * rows_seq + HIST_ROWS, :] = ext_ref[(s + 1) * rows_seq - HIST_ROWS:
                                                                    (s + 1) * rows_seq, :]


def _mixer(x, hist, mod, row_off, pos0, n_seq, wp_bf, pool_scale, ln_g, ln_b, ln_idx):
    bsz, seq, d = x.shape
    tile_t = min(seq, 256)
    assert bsz % n_seq == 0 and row_off % n_seq == 0 and tile_t % 8 == 0
    n = n_seq * (HIST_ROWS + tile_t)
    gd = POOL_GROUP_DIM
    mod_spec = lambda chunk: pl.BlockSpec((n_seq, 1, d), lambda b, t: (b + row_off // n_seq, 0, chunk))
    return pl.pallas_call(
        functools.partial(_mixer_kernel, pos0, tile_t, n_seq),
        grid=(bsz // n_seq, seq // tile_t),
        in_specs=[
            pl.BlockSpec((n_seq, tile_t, d), lambda b, t: (b, t, 0)),
            pl.BlockSpec((n_seq, POOL_HIST, d), lambda b, t: (b, 0, 0)),
            mod_spec(0), mod_spec(1), mod_spec(2),
            pl.BlockSpec(wp_bf.shape, lambda b, t: (0, 0, 0)),
            pl.BlockSpec((1, d), lambda b, t: (0, 0)),
            pl.BlockSpec((1, 1, d), lambda b, t: (ln_idx, 0, 0)),
            pl.BlockSpec((1, 1, d), lambda b, t: (ln_idx, 0, 0)),
        ],
        out_specs=[
            pl.BlockSpec((n_seq, tile_t, d), lambda b, t: (b, t, 0)),
            pl.BlockSpec((n_seq, POOL_HIST, d), lambda b, t: (b, 0, 0)),
        ],
        out_shape=[jax.ShapeDtypeStruct((bsz, seq, d), F32),
                   jax.ShapeDtypeStruct((bsz, POOL_HIST, d), F32)],
        scratch_shapes=[pltpu.VMEM((n, d), F32), pltpu.VMEM((n, d - gd), F32),
                        pltpu.VMEM((n, d - gd), F32), pltpu.VMEM((n, d - 2 * gd), F32)],
        compiler_params=_params("parallel", "arbitrary"),
        name="mixer",
    )(x, hist, mod, mod, mod, wp_bf, pool_scale, ln_g, ln_b)


def _mlp_kernel(has_oproj, *refs):
    if has_oproj:
        (x_ref, o_ref, g1_ref, wo_ref, lng1_ref, lnb1_ref,
         sh_ref, sc_ref, g_ref, wu_ref, wd_ref, lng_ref, lnb_ref, out_ref, acc_ref) = refs
        mix = jnp.dot(o_ref[0], wo_ref[...], preferred_element_type=F32)
        x = _layer_norm(DN_ALPHA * x_ref[0] + g1_ref[0] * mix, lng1_ref[0], lnb1_ref[0])
    else:
        (x_ref, sh_ref, sc_ref, g_ref, wu_ref, wd_ref, lng_ref, lnb_ref, out_ref, acc_ref) = refs
        x = x_ref[0]
    h = (x * (1.0 + sc_ref[0]) + sh_ref[0]).astype(BF16)
    for c in range(D_FF // MLP_TILE_F):
        lo, hi = c * MLP_TILE_F, (c + 1) * MLP_TILE_F
        u = jnp.dot(h, wu_ref[:, lo:hi], preferred_element_type=F32)
        u = jnp.maximum(u, 0.0)
        contrib = jnp.dot((u * u).astype(BF16), wd_ref[lo:hi, :], preferred_element_type=F32)
        if c == 0:
            acc_ref[...] = contrib
        else:
            acc_ref[...] += contrib
    y = DN_ALPHA * x + g_ref[0] * acc_ref[...]
    out_ref[0] = _layer_norm(y, lng_ref[0], lnb_ref[0])


def _mlp(x, mod, mod_rows, row_off, wu_bf, wd_bf, ln_g, ln_b, ln_idx, oproj=None):
    grp, rows, d = x.shape
    tile_m = min(rows, MLP_TILE_M)
    xspec = pl.BlockSpec((1, tile_m, d), lambda b, t: (b, t, 0))
    lnspec = lambda idx: pl.BlockSpec((1, 1, d), lambda b, t: (idx, 0, 0))
    ins, specs = [x], [xspec]
    if oproj is not None:
        o_bf, wo_bf, ln_idx1 = oproj
        ins += [o_bf, mod, wo_bf, ln_g, ln_b]
        specs += [xspec, _mod_spec(mod_rows, 2, row_off), _resident((d, d), lambda b, t: (0, 0)),
                  lnspec(ln_idx1), lnspec(ln_idx1)]
    ins += [mod, mod, mod, wu_bf, wd_bf, ln_g, ln_b]
    specs += [_mod_spec(mod_rows, 3, row_off), _mod_spec(mod_rows, 4, row_off), _mod_spec(mod_rows, 5, row_off),
              _resident((d, D_FF), lambda b, t: (0, 0)), _resident((D_FF, d), lambda b, t: (0, 0)),
              lnspec(ln_idx), lnspec(ln_idx)]
    return pl.pallas_call(
        functools.partial(_mlp_kernel, oproj is not None),
        grid=(grp, rows // tile_m),
        in_specs=specs,
        out_specs=xspec,
        out_shape=jax.ShapeDtypeStruct((grp, rows, d), F32),
        scratch_shapes=[pltpu.VMEM((tile_m, d), F32)],
        compiler_params=_params("parallel", "parallel"),
        name="mlp_oproj" if oproj is not None else "mlp",
    )(*ins)


def _kvq_kernel(decode, x_ref, sh_ref, sc_ref, wkv_ref, wq_ref, k_ref, v_ref, k2_ref, v2_ref, q_ref):
    x = x_ref[0]
    kv = jnp.dot(x.astype(BF16), wkv_ref[...], preferred_element_type=F32)
    k, v = kv[:, :D_MODEL], kv[:, D_MODEL:]
    k_ref[0] = k
    v_ref[0] = v
    if decode:
        k2_ref[0] = k.T
        v2_ref[0] = v.T
    else:
        k2_ref[0] = k.astype(BF16)
        blk = v2_ref.shape[3]
        for kb in range(v2_ref.shape[1]):
            v2_ref[0, kb] = v[kb * blk:(kb + 1) * blk, :].T.astype(BF16)
    h = (x * (1.0 + sc_ref[0]) + sh_ref[0]).astype(BF16)
    q_ref[0] = (jnp.dot(h, wq_ref[...], preferred_element_type=F32) * ATTN_SCALE).astype(q_ref.dtype)


def _kvq(x, mod, mod_rows, row_off, wkv_bf, wq_bf, decode):
    grp, rows, d = x.shape
    tile_m = min(rows, MLP_TILE_M)
    xspec = pl.BlockSpec((1, tile_m, d), lambda b, t: (b, t, 0))
    sds = lambda dt: jax.ShapeDtypeStruct((grp, rows, d), dt)
    if decode:
        tspec = pl.BlockSpec((1, d, tile_m), lambda b, t: (b, 0, t))
        specs2, sds2 = [tspec, tspec], [jax.ShapeDtypeStruct((grp, d, rows), F32)] * 2
    else:
        blk = min(rows, ATTN_BLOCK)
        specs2 = [xspec, pl.BlockSpec((1, tile_m // blk, d, blk), lambda b, t: (b, t, 0, 0))]
        sds2 = [sds(BF16), jax.ShapeDtypeStruct((grp, rows // blk, d, blk), BF16)]
    return pl.pallas_call(
        functools.partial(_kvq_kernel, decode),
        grid=(grp, rows // tile_m),
        in_specs=[xspec, _mod_spec(mod_rows, 0, row_off), _mod_spec(mod_rows, 1, row_off),
                  _resident((d, 2 * d), lambda b, t: (0, 0)), _resident((d, d), lambda b, t: (0, 0))],
        out_specs=[xspec, xspec] + specs2 + [xspec],
        out_shape=[sds(F32), sds(F32)] + sds2 + [sds(F32 if decode else BF16)],
        compiler_params=_params("parallel", "parallel"),
        name="kvq",
    )(x, mod, mod, wkv_bf, wq_bf)


SOFTPLUS_LINEAR_FROM = 30.0


def _softplus(z):
    return jnp.maximum(jnp.log(1.0 + jnp.exp(jnp.minimum(z, SOFTPLUS_LINEAR_FROM))), z)


def _triangle(n, transposed=False):
    r = lax.broadcasted_iota(jnp.int32, (n, n), 0)
    c = lax.broadcasted_iota(jnp.int32, (n, n), 1)
    return jnp.where((c > r) if transposed else (r > c), 1.0, 0.0).astype(BF16)


def _attn_kernel(blk, bias_ref, q_ref, k_ref, vt_ref, o_ref):
    group = pl.program_id(1)
    i = pl.program_id(2)
    lane = lax.broadcasted_iota(jnp.int32, (1, SLAB), 1)
    key = lax.broadcasted_iota(jnp.int32, (blk, blk), 0)
    qry = lax.broadcasted_iota(jnp.int32, (blk, blk), 1)
    causal = key < qry
    tri = _triangle(blk, transposed=True)
    heads = [(sl, hh) for sl in range(ATTN_SLABS_PER_STEP) for hh in range(HEADS_PER_SLAB)]
    qms, biases = [], []
    for sl, hh in heads:
        q = q_ref[0, :, sl * SLAB:(sl + 1) * SLAB]
        in_head = (lane >= hh * HEAD_DIM) & (lane < (hh + 1) * HEAD_DIM)
        qms.append(jnp.where(in_head, q, jnp.zeros_like(q)))
        biases.append(bias_ref[(group * ATTN_SLABS_PER_STEP + sl) * HEADS_PER_SLAB + hh])

    def tile(j, carry, mask):
        start = pl.multiple_of(j * blk, blk)
        new = []
        for w0 in range(0, len(heads), ATTN_HEADS_PER_WAVE):
            wave = list(range(w0, min(w0 + ATTN_HEADS_PER_WAVE, len(heads))))
            zs = [lax.dot_general(k_ref[0, pl.ds(start, blk), heads[c][0] * SLAB:(heads[c][0] + 1) * SLAB],
                                  qms[c], (((1,), (1,)), ((), ())), preferred_element_type=F32) + biases[c]
                  for c in wave]
            nks = [_softplus(z) for z in zs]
            if mask is not None:
                nks = [jnp.where(mask, nk, 0.0) for nk in nks]
            log_betas = [z - nk for z, nk in zip(zs, nks)]
            first = [nk[0:1, :] for nk in nks]
            excls = [jnp.dot(tri, nk.astype(BF16), preferred_element_type=F32) for nk in nks]
            ws = [jnp.exp(lb - excl - carry[c][0]) for c, lb, excl in zip(wave, log_betas, excls)]
            if mask is not None:
                ws = [jnp.where(mask, a, 0.0) for a in ws]
            for n, c in enumerate(wave):
                lo = heads[c][0] * SLAB + heads[c][1] * HEAD_DIM
                vt = vt_ref[0, j, lo:lo + HEAD_DIM, :]
                new.append((carry[c][0] + (excls[n][0:1, :] + first[n]),
                            carry[c][1] + jnp.dot(vt, ws[n].astype(BF16), preferred_element_type=F32)))
        return tuple(new)

    init = tuple((jnp.zeros((1, blk), F32), jnp.zeros((HEAD_DIM, blk), F32)) for _ in heads)
    carry = tile(i, init, causal)
    carry = lax.fori_loop(0, i, lambda n, c: tile(i - 1 - n, c, None), carry)
    for sl in range(ATTN_SLABS_PER_STEP):
        ot = jnp.concatenate([carry[HEADS_PER_SLAB * sl + hh][1] for hh in range(HEADS_PER_SLAB)], axis=0)
        o_ref[0, :, sl * SLAB:(sl + 1) * SLAB] = ot.T.astype(o_ref.dtype)


def _attn_prompt(q_bf, k_bf, vt_bf, bias):
    bsz, seq, d = q_bf.shape
    blk = vt_bf.shape[3]
    width = ATTN_SLABS_PER_STEP * SLAB
    qspec = pl.BlockSpec((1, blk, width), lambda b, p, i: (b, i, p))
    kspec = pl.BlockSpec((1, seq, width), lambda b, p, i: (b, 0, p))
    vspec = pl.BlockSpec((1, seq // blk, width, blk), lambda b, p, i: (b, 0, p, 0))
    return pl.pallas_call(
        functools.partial(_attn_kernel, blk),
        grid=(bsz, N_SLABS // ATTN_SLABS_PER_STEP, seq // blk),
        in_specs=[pl.BlockSpec(memory_space=pltpu.SMEM), qspec, kspec, vspec],
        out_specs=qspec,
        out_shape=jax.ShapeDtypeStruct((bsz, seq, d), BF16),
        compiler_params=_params("parallel", "parallel", "arbitrary"),
        name="attn_prompt",
    )(bias, q_bf, k_bf, vt_bf)


def _decode_kernel(n_pages_step, *refs):
    pt_ref = refs[0]
    q_ref, bias_ref, qpos_ref, kn_ref, vn_ref = refs[1:6]
    k_refs = refs[6:6 + n_pages_step]
    v_refs = refs[6 + n_pages_step:6 + 2 * n_pages_step]
    o_ref, r_ref, acc_ref = refs[6 + 2 * n_pages_step:]
    del pt_ref
    step = pl.program_id(1)
    t_new = q_ref.shape[1]
    rows = N_SLABS * HEADS_PER_SLAB * t_new
    slab_rows = HEADS_PER_SLAB * t_new

    q = q_ref[0]
    lane = lax.broadcasted_iota(jnp.int32, (1, SLAB), 1)
    qms = []
    for p in range(N_SLABS):
        qp = q[:, p * SLAB:(p + 1) * SLAB]
        parts = [jnp.where((lane >= hh * HEAD_DIM) & (lane < (hh + 1) * HEAD_DIM), qp, 0.0)
                 for hh in range(HEADS_PER_SLAB)]
        qms.append(jnp.concatenate(parts, axis=0).astype(BF16))
    bias = bias_ref[...]
    tri = _triangle(PAGE_SIZE)

    def pages(kts, vts, mask):
        zs = [jnp.concatenate(
            [jnp.dot(qms[p], kt[0, p * SLAB:(p + 1) * SLAB, :].astype(BF16), preferred_element_type=F32)
             for p in range(N_SLABS)], axis=0) + bias for kt in kts]
        nks = [_softplus(z) for z in zs]
        if mask is not None:
            nks = [jnp.where(mask, nk, 0.0) for nk in nks]
        excls = [jnp.dot(nk.astype(BF16), tri, preferred_element_type=F32) for nk in nks]
        r = r_ref[...]
        ws = []
        for z, nk, excl in zip(zs, nks, excls):
            a = jnp.exp((z - nk) - excl - r)
            if mask is not None:
                a = jnp.where(mask, a, 0.0)
            ws.append(a.astype(BF16))
            r = r + (excl[:, 0:1] + nk[:, 0:1])
        r_ref[...] = r
        for p in range(N_SLABS):
            part = acc_ref[p * slab_rows:(p + 1) * slab_rows, :]
            for w, vt in zip(ws, vts):
                part = part + lax.dot_general(
                    w[p * slab_rows:(p + 1) * slab_rows, :], vt[0, p * SLAB:(p + 1) * SLAB, :].astype(BF16),
                    (((1,), (1,)), ((), ())), preferred_element_type=F32)
            acc_ref[p * slab_rows:(p + 1) * slab_rows, :] = part

    @pl.when(step == 0)
    def _():
        r_ref[...] = jnp.zeros_like(r_ref)
        acc_ref[...] = jnp.zeros_like(acc_ref)
        off = (pl.program_id(0) * t_new) % PAGE_SIZE
        key = lax.broadcasted_iota(jnp.int32, (rows, PAGE_SIZE), 1) - off
        pages([kn_ref], [vn_ref], (key >= 0) & (key < qpos_ref[...]))

    order = list(reversed(range(n_pages_step)))
    pages([k_refs[g] for g in order], [v_refs[g] for g in order], None)

    @pl.when(step == pl.num_programs(1) - 1)
    def _():
        for p in range(N_SLABS):
            blk = acc_ref[p * slab_rows:(p + 1) * slab_rows, :]
            o_ref[0, :, p * SLAB:(p + 1) * SLAB] = jnp.where(
                lane < HEAD_DIM, blk[:t_new], blk[t_new:]).astype(o_ref.dtype)


def _attn_decode(q, kt_new, vt_new, cache_k, cache_v, page_table, bias_rows, qpos_rows):
    bsz, t_new, d = q.shape
    assert PAGE_SIZE % t_new == 0 and kt_new.shape[2] % PAGE_SIZE == 0
    n_pages = page_table.shape[1]
    g = DECODE_PAGES_PER_STEP
    while n_pages % g:
        g //= 2
    n_steps = n_pages // g
    rows = N_HEADS * t_new

    def page_spec(slot):
        return pl.BlockSpec((1, d, PAGE_SIZE), lambda b, s, pt: (pt[b, (n_steps - 1 - s) * g + slot], 0, 0))

    per_b = lambda shape: pl.BlockSpec(shape, lambda b, s, pt: (b, 0, 0))
    new_spec = pl.BlockSpec((1, d, PAGE_SIZE), lambda b, s, pt: (0, 0, (b * t_new) // PAGE_SIZE))
    grid_spec = pltpu.PrefetchScalarGridSpec(
        num_scalar_prefetch=1,
        grid=(bsz, n_steps),
        in_specs=[per_b((1, t_new, d)), pl.BlockSpec((rows, 1), lambda b, s, pt: (0, 0)),
                  pl.BlockSpec((rows, 1), lambda b, s, pt: (0, 0)), new_spec, new_spec]
                 + [page_spec(slot) for slot in range(g)] * 2,
        out_specs=per_b((1, t_new, d)),
        scratch_shapes=[pltpu.VMEM((rows, 1), F32), pltpu.VMEM((rows, SLAB), F32)],
    )
    return pl.pallas_call(
        functools.partial(_decode_kernel, g),
        grid_spec=grid_spec,
        out_shape=jax.ShapeDtypeStruct((bsz, t_new, d), BF16),
        compiler_params=_params("parallel", "arbitrary"),
        name="attn_decode",
    )(page_table, q, bias_rows, qpos_rows, kt_new, vt_new, *([cache_k] * g), *([cache_v] * g))


def kernel(x_prompt, x_sample, cache_k, cache_v, state_pool, page_table, c_prompt, c_sample,
           w_ada, b_ada, ln_g, ln_b, w_pool, pool_scale, w_up, w_down, w_kv, w_q, w_o, b_break):
    bsz, seq, d = x_prompt.shape
    dec_b, dec_t, _ = x_sample.shape
    assert d == D_MODEL and w_ada.shape[0] == DEPTH == 2 and w_pool.shape[0] == 1 and w_q.shape[0] == 1
    past_len = page_table.shape[1] * PAGE_SIZE

    wu_bf, wd_bf = w_up.astype(BF16), w_down.astype(BF16)
    wkv_bf, wq_bf, wo_bf = w_kv.astype(BF16), w_q[0].astype(BF16), w_o[0].astype(BF16)
    wp_bf = w_pool[0].astype(BF16)
    ln_g4 = ln_g.reshape(DEPTH * 2, 1, d)
    ln_b4 = ln_b.reshape(DEPTH * 2, 1, d)

    n_tok = dec_b * dec_t
    ada = _ada(jnp.concatenate([jnp.repeat(c_sample, dec_t, axis=0), c_prompt], axis=0), w_ada, b_ada)
    mod_p = [ada[l].reshape(n_tok + bsz, 1, 6 * d) for l in range(DEPTH)]
    mod_s = [ada[l][None] for l in range(DEPTH)]
    mod_seq = ada[0, 0:n_tok:dec_t].reshape(dec_b, 1, 6 * d)
    xs_flat = lambda a: a.reshape(1, n_tok, d)

    zero_hist = jnp.zeros((bsz, POOL_HIST, d), F32)
    n_seq = MIXER_SEQS_PER_STEP
    while dec_b % n_seq:
        n_seq //= 2
    x1_p, hist_p = _mixer(x_prompt, zero_hist, mod_p[0], n_tok, 0, 1, wp_bf, pool_scale, ln_g4, ln_b4, 0)
    x1_s, hist_s = _mixer(x_sample, state_pool[0], mod_seq, 0, past_len, n_seq, wp_bf, pool_scale,
                          ln_g4, ln_b4, 0)
    x2_p = _mlp(x1_p, mod_p[0], 1, n_tok, wu_bf[0], wd_bf[0], ln_g4, ln_b4, 1)
    x2_s = _mlp(xs_flat(x1_s), mod_s[0], n_tok, 0, wu_bf[0], wd_bf[0], ln_g4, ln_b4, 1)

    k_p, v_p, kb_p, vtb_p, q_p = _kvq(x2_p, mod_p[1], 1, n_tok, wkv_bf, wq_bf, decode=False)
    k_s, v_s, kt_s, vt_s, q_s = _kvq(x2_s, mod_s[1], n_tok, 0, wkv_bf, wq_bf, decode=True)

    bias = b_break[0].astype(F32)
    o_p = _attn_prompt(q_p, kb_p, vtb_p, bias)
    pages = lambda c: jnp.transpose(c, (0, 2, 3, 1)).reshape(-1, d, PAGE_SIZE)
    pad_cols = -n_tok % PAGE_SIZE
    if pad_cols:
        kt_s, vt_s = (jnp.pad(a, ((0, 0), (0, 0), (0, pad_cols))) for a in (kt_s, vt_s))
    o_s = _attn_decode(q_s.reshape(dec_b, dec_t, d), kt_s, vt_s, pages(cache_k), pages(cache_v),
                       page_table, jnp.repeat(bias, dec_t)[:, None],
                       jnp.tile(jnp.arange(dec_t, dtype=jnp.int32), N_HEADS)[:, None])

    y_p = _mlp(x2_p, mod_p[1], 1, n_tok, wu_bf[1], wd_bf[1], ln_g4, ln_b4, 3, oproj=(o_p, wo_bf, 2))
    y_s = _mlp(x2_s, mod_s[1], n_tok, 0, wu_bf[1], wd_bf[1], ln_g4, ln_b4, 3,
               oproj=(xs_flat(o_s), wo_bf, 2))

    heads = lambda a, b, t: a.reshape(b, t, N_HEADS, HEAD_DIM)
    return (y_p, y_s.reshape(dec_b, dec_t, d),
            heads(k_p, bsz, seq), heads(v_p, bsz, seq), hist_p[None],
            heads(k_s, dec_b, dec_t), heads(v_s, dec_b, dec_t), hist_s[None])
```

```python
import functools

import jax
import jax.numpy as jnp
from jax import lax
from jax.experimental import pallas as pl
from jax.experimental.pallas import tpu as pltpu

F32 = jnp.float32
BF16 = jnp.bfloat16

D_MODEL = 1024
DEPTH = 2
N_HEADS = 16
HEAD_DIM = 64
HEADS_PER_SLAB = 2
SLAB = HEADS_PER_SLAB * HEAD_DIM
N_SLABS = N_HEADS // HEADS_PER_SLAB
D_FF = 4 * D_MODEL
POOL_WINDOWS = (2, 4, 8, 16)
POOL_GROUP_DIM = D_MODEL // len(POOL_WINDOWS)
POOL_HIST = max(POOL_WINDOWS) - 1
HIST_ROWS = 2 * max(POOL_WINDOWS)
MOD_ROW_GROUP = 8
MIXER_SEQS_PER_STEP = 8
LN_EPS = 1e-5
DN_ALPHA = float((2 * DEPTH) ** 0.25)
ATTN_SCALE = HEAD_DIM ** -0.5
PAGE_SIZE = 128

VMEM_LIMIT_BYTES = 56 * 1024 * 1024
ADA_TILE_N = 1536
MLP_TILE_M = 512
MLP_TILE_F = 512
ATTN_BLOCK = 256
ATTN_SLABS_PER_STEP = 4
ATTN_HEADS_PER_WAVE = 8
DECODE_PAGES_PER_STEP = 16


def _params(*sem):
    return pltpu.CompilerParams(dimension_semantics=sem, vmem_limit_bytes=VMEM_LIMIT_BYTES)


def _layer_norm(y, g, b):
    mu = jnp.mean(y, axis=-1, keepdims=True)
    yc = y - mu
    var = jnp.mean(yc * yc, axis=-1, keepdims=True)
    return yc * lax.rsqrt(var + LN_EPS) * g + b


def _resident(shape, index_map):
    return pl.BlockSpec(shape, index_map, pipeline_mode=pl.Buffered(1))


def _ada_kernel(c_ref, w_ref, b_ref, o_ref):
    c = c_ref[...]
    s = c * jax.nn.sigmoid(c)
    o_ref[0] = jnp.dot(s.astype(BF16), w_ref[0].astype(BF16), preferred_element_type=F32) + b_ref[0]


def _ada(c_all, w_ada, b_ada):
    n_layers, d, n = w_ada.shape
    m = c_all.shape[0]
    return pl.pallas_call(
        _ada_kernel,
        grid=(n_layers, n // ADA_TILE_N),
        in_specs=[
            pl.BlockSpec((m, d), lambda l, j: (0, 0)),
            pl.BlockSpec((1, d, ADA_TILE_N), lambda l, j: (l, 0, j)),
            pl.BlockSpec((1, 1, ADA_TILE_N), lambda l, j: (l, 0, j)),
        ],
        out_specs=pl.BlockSpec((1, m, ADA_TILE_N), lambda l, j: (l, 0, j)),
        out_shape=jax.ShapeDtypeStruct((n_layers, m, n), F32),
        compiler_params=_params("parallel", "parallel"),
        name="ada",
    )(c_all, w_ada, b_ada.reshape(n_layers, 1, n))


class _ModRows:
    def __init__(self, layer, rows, first_row, per_step):
        self.layer, self.rows, self.first_row, self.per_step = layer, rows, first_row, per_step

    def spec(self, chunk):
        return pl.BlockSpec((1, self.rows, D_MODEL),
                            lambda b, t: (self.layer, self.first_row(b) // self.rows, chunk))

    def read(self, ref, s=0):
        if self.per_step is None:
            return ref[0]
        return ref[0, pl.ds((pl.program_id(0) * self.per_step) % self.rows + s, 1), :]


def _mixer_kernel(pos0, tile_t, n_seq, mod, x_ref, hist_ref, sh_ref, sc_ref, g_ref, wp_ref, ps_ref,
                  lng_ref, lnb_ref, o_ref, ho_ref, ext_ref, s2_ref, s4_ref, s8_ref):
    t = pl.program_id(1)
    gd = POOL_GROUP_DIM
    rows_seq = HIST_ROWS + tile_t
    n = n_seq * rows_seq

    @pl.when(t == 0)
    def _():
        for s in range(n_seq):
            ext_ref[s * rows_seq:s * rows_seq + HIST_ROWS - POOL_HIST, :] = jnp.zeros(
                (HIST_ROWS - POOL_HIST, D_MODEL), F32)
            ext_ref[s * rows_seq + HIST_ROWS - POOL_HIST:s * rows_seq + HIST_ROWS, :] = hist_ref[s]

    for s in range(n_seq):
        ext_ref[s * rows_seq + HIST_ROWS:(s + 1) * rows_seq, :] = (
            x_ref[s] * (1.0 + mod.read(sc_ref, s)) + mod.read(sh_ref, s))
    s2_ref[8:n, :] = ext_ref[8:n, gd:] + ext_ref[7:n - 1, gd:]
    s4_ref[16:n, :] = s2_ref[16:n, :] + s2_ref[14:n - 2, :]
    s8_ref[24:n, :] = s4_ref[24:n, gd:] + s4_ref[20:n - 4, gd:]

    pos = pos0 + t * tile_t + lax.broadcasted_iota(jnp.int32, (tile_t, 1), 0)
    inv_cnt = [1.0 / jnp.minimum(w, pos + 1).astype(F32) for w in POOL_WINDOWS]
    ds = [[] for _ in POOL_WINDOWS]
    for s in range(n_seq):
        a, b = s * rows_seq + HIST_ROWS, (s + 1) * rows_seq
        wsums = [ext_ref[a:b, :gd] + ext_ref[a - 1:b - 1, :gd], s4_ref[a:b, :gd], s8_ref[a:b, :gd],
                 s8_ref[a:b, gd:] + s8_ref[a - 8:b - 8, gd:]]
        for g, wsum in enumerate(wsums):
            ds[g].append(wsum * inv_cnt[g] - ext_ref[a:b, g * gd:(g + 1) * gd])
    mix = jnp.concatenate(
        [jnp.dot(jnp.concatenate(ds[g], axis=0).astype(BF16), wp_ref[0, g], preferred_element_type=F32)
         for g in range(len(POOL_WINDOWS))], axis=-1) * ps_ref[...]
    for s in range(n_seq):
        y = DN_ALPHA * x_ref[s] + mod.read(g_ref, s) * mix[s * tile_t:(s + 1) * tile_t]
        o_ref[s] = _layer_norm(y, lng_ref[0], lnb_ref[0])
        ho_ref[s] = ext_ref[(s + 1) * rows_seq - POOL_HIST:(s + 1) * rows_seq, :]
        ext_ref[s * rows_seq:s * rows_seq + HIST_ROWS, :] = ext_ref[(s + 1) * rows_seq - HIST_ROWS:
                                                                    (s + 1) * rows_seq, :]


def _mixer(x, hist, ada, mod, pos0, wp_bf, pool_scale, ln_g, ln_b, ln_idx):
    bsz, seq, d = x.shape
    tile_t = min(seq, 256)
    n_seq = mod.per_step
    assert bsz % n_seq == 0 and tile_t % 8 == 0
    n = n_seq * (HIST_ROWS + tile_t)
    gd = POOL_GROUP_DIM
    return pl.pallas_call(
        functools.partial(_mixer_kernel, pos0, tile_t, n_seq, mod),
        grid=(bsz // n_seq, seq // tile_t),
        in_specs=[
            pl.BlockSpec((n_seq, tile_t, d), lambda b, t: (b, t, 0)),
            pl.BlockSpec((n_seq, POOL_HIST, d), lambda b, t: (b, 0, 0)),
            mod.spec(0), mod.spec(1), mod.spec(2),
            pl.BlockSpec(wp_bf.shape, lambda b, t: (0, 0, 0, 0)),
            pl.BlockSpec((1, d), lambda b, t: (0, 0)),
            pl.BlockSpec((1, 1, d), lambda b, t: (ln_idx, 0, 0)),
            pl.BlockSpec((1, 1, d), lambda b, t: (ln_idx, 0, 0)),
        ],
        out_specs=[
            pl.BlockSpec((n_seq, tile_t, d), lambda b, t: (b, t, 0)),
            pl.BlockSpec((n_seq, POOL_HIST, d), lambda b, t: (b, 0, 0)),
        ],
        out_shape=[jax.ShapeDtypeStruct((bsz, seq, d), F32),
                   jax.ShapeDtypeStruct((bsz, POOL_HIST, d), F32)],
        scratch_shapes=[pltpu.VMEM((n, d), F32), pltpu.VMEM((n, d - gd), F32),
                        pltpu.VMEM((n, d - gd), F32), pltpu.VMEM((n, d - 2 * gd), F32)],
        compiler_params=_params("parallel", "arbitrary"),
        name="mixer",
    )(x, hist, ada, ada, ada, wp_bf, pool_scale, ln_g, ln_b)


def _mlp_kernel(has_oproj, mod, *refs):
    if has_oproj:
        (x_ref, o_ref, g1_ref, wo_ref, lng1_ref, lnb1_ref,
         sh_ref, sc_ref, g_ref, wu_ref, wd_ref, lng_ref, lnb_ref, out_ref, acc_ref) = refs
        mix = jnp.dot(o_ref[0], wo_ref[0], preferred_element_type=F32)
        x = _layer_norm(DN_ALPHA * x_ref[0] + mod.read(g1_ref) * mix, lng1_ref[0], lnb1_ref[0])
    else:
        (x_ref, sh_ref, sc_ref, g_ref, wu_ref, wd_ref, lng_ref, lnb_ref, out_ref, acc_ref) = refs
        x = x_ref[0]
    h = (x * (1.0 + mod.read(sc_ref)) + mod.read(sh_ref)).astype(BF16)
    for c in range(D_FF // MLP_TILE_F):
        lo, hi = c * MLP_TILE_F, (c + 1) * MLP_TILE_F
        u = jnp.dot(h, wu_ref[0, :, lo:hi], preferred_element_type=F32)
        u = jnp.maximum(u, 0.0)
        contrib = jnp.dot((u * u).astype(BF16), wd_ref[0, lo:hi, :], preferred_element_type=F32)
        if c == 0:
            acc_ref[...] = contrib
        else:
            acc_ref[...] += contrib
    y = DN_ALPHA * x + mod.read(g_ref) * acc_ref[...]
    out_ref[0] = _layer_norm(y, lng_ref[0], lnb_ref[0])


def _mlp(x, ada, mod, wu_bf, wd_bf, ln_g, ln_b, ln_idx, oproj=None):
    grp, rows, d = x.shape
    tile_m = min(rows, MLP_TILE_M)
    xspec = pl.BlockSpec((1, tile_m, d), lambda b, t: (b, t, 0))
    lnspec = lambda idx: pl.BlockSpec((1, 1, d), lambda b, t: (idx, 0, 0))
    ins, specs = [x], [xspec]
    if oproj is not None:
        o_bf, wo_bf, ln_idx1 = oproj
        ins += [o_bf, ada, wo_bf, ln_g, ln_b]
        specs += [xspec, mod.spec(2), _resident((1, d, d), lambda b, t: (0, 0, 0)),
                  lnspec(ln_idx1), lnspec(ln_idx1)]
    ins += [ada, ada, ada, wu_bf, wd_bf, ln_g, ln_b]
    specs += [mod.spec(3), mod.spec(4), mod.spec(5),
              _resident((1, d, D_FF), lambda b, t: (mod.layer, 0, 0)),
              _resident((1, D_FF, d), lambda b, t: (mod.layer, 0, 0)),
              lnspec(ln_idx), lnspec(ln_idx)]
    return pl.pallas_call(
        functools.partial(_mlp_kernel, oproj is not None, mod),
        grid=(grp, rows // tile_m),
        in_specs=specs,
        out_specs=xspec,
        out_shape=jax.ShapeDtypeStruct((grp, rows, d), F32),
        scratch_shapes=[pltpu.VMEM((tile_m, d), F32)],
        compiler_params=_params("parallel", "parallel"),
        name="mlp_oproj" if oproj is not None else "mlp",
    )(*ins)


def _kvq_kernel(decode, mod, x_ref, sh_ref, sc_ref, wkv_ref, wq_ref, k_ref, v_ref, k2_ref, v2_ref, q_ref):
    x = x_ref[0]
    kv = jnp.dot(x.astype(BF16), wkv_ref[...], preferred_element_type=F32)
    k, v = kv[:, :D_MODEL], kv[:, D_MODEL:]
    k_ref[0] = k
    v_ref[0] = v
    if decode:
        k2_ref[0] = k.T
        v2_ref[0] = v.T
    else:
        k2_ref[0] = k.astype(BF16)
        blk = v2_ref.shape[3]
        for kb in range(v2_ref.shape[1]):
            v2_ref[0, kb] = v[kb * blk:(kb + 1) * blk, :].T.astype(BF16)
    h = (x * (1.0 + mod.read(sc_ref)) + mod.read(sh_ref)).astype(BF16)
    q_ref[0] = (jnp.dot(h, wq_ref[0], preferred_element_type=F32) * ATTN_SCALE).astype(q_ref.dtype)


def _kvq(x, ada, mod, wkv_bf, wq_bf, decode):
    grp, rows, d = x.shape
    tile_m = min(rows, MLP_TILE_M)
    xspec = pl.BlockSpec((1, tile_m, d), lambda b, t: (b, t, 0))
    sds = lambda dt: jax.ShapeDtypeStruct((grp, rows, d), dt)
    if decode:
        tspec = pl.BlockSpec((1, d, tile_m), lambda b, t: (b, 0, t))
        specs2, sds2 = [tspec, tspec], [jax.ShapeDtypeStruct((grp, d, rows), F32)] * 2
    else:
        blk = min(rows, ATTN_BLOCK)
        specs2 = [xspec, pl.BlockSpec((1, tile_m // blk, d, blk), lambda b, t: (b, t, 0, 0))]
        sds2 = [sds(BF16), jax.ShapeDtypeStruct((grp, rows // blk, d, blk), BF16)]
    return pl.pallas_call(
        functools.partial(_kvq_kernel, decode, mod),
        grid=(grp, rows // tile_m),
        in_specs=[xspec, mod.spec(0), mod.spec(1),
                  _resident((d, 2 * d), lambda b, t: (0, 0)), _resident((1, d, d), lambda b, t: (0, 0, 0))],
        out_specs=[xspec, xspec] + specs2 + [xspec],
        out_shape=[sds(F32), sds(F32)] + sds2 + [sds(F32 if decode else BF16)],
        compiler_params=_params("parallel", "parallel"),
        name="kvq",
    )(x, ada, ada, wkv_bf, wq_bf)


SOFTPLUS_LINEAR_FROM = 30.0


def _softplus(z):
    return jnp.maximum(jnp.log(1.0 + jnp.exp(jnp.minimum(z, SOFTPLUS_LINEAR_FROM))), z)


def _triangle(n, transposed=False):
    r = lax.broadcasted_iota(jnp.int32, (n, n), 0)
    c = lax.broadcasted_iota(jnp.int32, (n, n), 1)
    return jnp.where((c > r) if transposed else (r > c), 1.0, 0.0).astype(BF16)


def _attn_kernel(blk, bias_ref, q_ref, k_ref, vt_ref, o_ref):
    group = pl.program_id(1)
    i = pl.program_id(2)
    lane = lax.broadcasted_iota(jnp.int32, (1, SLAB), 1)
    key = lax.broadcasted_iota(jnp.int32, (blk, blk), 0)
    qry = lax.broadcasted_iota(jnp.int32, (blk, blk), 1)
    causal = key < qry
    tri = _triangle(blk, transposed=True)
    heads = [(sl, hh) for sl in range(ATTN_SLABS_PER_STEP) for hh in range(HEADS_PER_SLAB)]
    qms, biases = [], []
    for sl, hh in heads:
        q = q_ref[0, :, sl * SLAB:(sl + 1) * SLAB]
        in_head = (lane >= hh * HEAD_DIM) & (lane < (hh + 1) * HEAD_DIM)
        qms.append(jnp.where(in_head, q, jnp.zeros_like(q)))
        biases.append(bias_ref[(group * ATTN_SLABS_PER_STEP + sl) * HEADS_PER_SLAB + hh])

    def tile(j, carry, mask):
        start = pl.multiple_of(j * blk, blk)
        new = []
        for w0 in range(0, len(heads), ATTN_HEADS_PER_WAVE):
            wave = list(range(w0, min(w0 + ATTN_HEADS_PER_WAVE, len(heads))))
            zs = [lax.dot_general(k_ref[0, pl.ds(start, blk), heads[c][0] * SLAB:(heads[c][0] + 1) * SLAB],
                                  qms[c], (((1,), (1,)), ((), ())), preferred_element_type=F32) + biases[c]
                  for c in wave]
            nks = [_softplus(z) for z in zs]
            if mask is not None:
                nks = [jnp.where(mask, nk, 0.0) for nk in nks]
            log_betas = [z - nk for z, nk in zip(zs, nks)]
            first = [nk[0:1, :] for nk in nks]
            excls = [jnp.dot(tri, nk.astype(BF16), preferred_element_type=F32) for nk in nks]
            ws = [jnp.exp(lb - excl - carry[c][0]) for c, lb, excl in zip(wave, log_betas, excls)]
            if mask is not None:
                ws = [jnp.where(mask, a, 0.0) for a in ws]
            for n, c in enumerate(wave):
                lo = heads[c][0] * SLAB + heads[c][1] * HEAD_DIM
                vt = vt_ref[0, j, lo:lo + HEAD_DIM, :]
                new.append((carry[c][0] + (excls[n][0:1, :] + first[n]),
                            carry[c][1] + jnp.dot(vt, ws[n].astype(BF16), preferred_element_type=F32)))
        return tuple(new)

    init = tuple((jnp.zeros((1, blk), F32), jnp.zeros((HEAD_DIM, blk), F32)) for _ in heads)
    carry = tile(i, init, causal)
    carry = lax.fori_loop(0, i, lambda n, c: tile(i - 1 - n, c, None), carry)
    for sl in range(ATTN_SLABS_PER_STEP):
        ot = jnp.concatenate([carry[HEADS_PER_SLAB * sl + hh][1] for hh in range(HEADS_PER_SLAB)], axis=0)
        o_ref[0, :, sl * SLAB:(sl + 1) * SLAB] = ot.T.astype(o_ref.dtype)


def _attn_prompt(q_bf, k_bf, vt_bf, bias):
    bsz, seq, d = q_bf.shape
    blk = vt_bf.shape[3]
    width = ATTN_SLABS_PER_STEP * SLAB
    qspec = pl.BlockSpec((1, blk, width), lambda b, p, i: (b, i, p))
    kspec = pl.BlockSpec((1, seq, width), lambda b, p, i: (b, 0, p))
    vspec = pl.BlockSpec((1, seq // blk, width, blk), lambda b, p, i: (b, 0, p, 0))
    return pl.pallas_call(
        functools.partial(_attn_kernel, blk),
        grid=(bsz, N_SLABS // ATTN_SLABS_PER_STEP, seq // blk),
        in_specs=[pl.BlockSpec(memory_space=pltpu.SMEM), qspec, kspec, vspec],
        out_specs=qspec,
        out_shape=jax.ShapeDtypeStruct((bsz, seq, d), BF16),
        compiler_params=_params("parallel", "parallel", "arbitrary"),
        name="attn_prompt",
    )(bias, q_bf, k_bf, vt_bf)


def _decode_kernel(n_pages_step, *refs):
    pt_ref = refs[0]
    q_ref, bias_ref, qpos_ref, kn_ref, vn_ref = refs[1:6]
    k_refs = refs[6:6 + n_pages_step]
    v_refs = refs[6 + n_pages_step:6 + 2 * n_pages_step]
    o_ref, r_ref, acc_ref = refs[6 + 2 * n_pages_step:]
    del pt_ref
    step = pl.program_id(1)
    t_new = q_ref.shape[1]
    rows = N_SLABS * HEADS_PER_SLAB * t_new
    slab_rows = HEADS_PER_SLAB * t_new

    q = q_ref[0]
    lane = lax.broadcasted_iota(jnp.int32, (1, SLAB), 1)
    qms = []
    for p in range(N_SLABS):
        qp = q[:, p * SLAB:(p + 1) * SLAB]
        parts = [jnp.where((lane >= hh * HEAD_DIM) & (lane < (hh + 1) * HEAD_DIM), qp, 0.0)
                 for hh in range(HEADS_PER_SLAB)]
        qms.append(jnp.concatenate(parts, axis=0).astype(BF16))
    bias = bias_ref[...]
    tri = _triangle(PAGE_SIZE)

    def pages(kts, vts, mask):
        zs = [jnp.concatenate(
            [jnp.dot(qms[p], kt[0, p * SLAB:(p + 1) * SLAB, :].astype(BF16), preferred_element_type=F32)
             for p in range(N_SLABS)], axis=0) + bias for kt in kts]
        nks = [_softplus(z) for z in zs]
        if mask is not None:
            nks = [jnp.where(mask, nk, 0.0) for nk in nks]
        excls = [jnp.dot(nk.astype(BF16), tri, preferred_element_type=F32) for nk in nks]
        r = r_ref[...]
        ws = []
        for z, nk, excl in zip(zs, nks, excls):
            a = jnp.exp((z - nk) - excl - r)
            if mask is not None:
                a = jnp.where(mask, a, 0.0)
            ws.append(a.astype(BF16))
            r = r + (excl[:, 0:1] + nk[:, 0:1])
        r_ref[...] = r
        for p in range(N_SLABS):
            part = acc_ref[p * slab_rows:(p + 1) * slab_rows, :]
            for w, vt in zip(ws, vts):
                part = part + jnp.dot(
                    w[p * slab_rows:(p + 1) * slab_rows, :], vt[0, p * SLAB:(p + 1) * SLAB, :].T.astype(BF16),
                    preferred_element_type=F32)
            acc_ref[p * slab_rows:(p + 1) * slab_rows, :] = part

    @pl.when(step == 0)
    def _():
        r_ref[...] = jnp.zeros_like(r_ref)
        acc_ref[...] = jnp.zeros_like(acc_ref)
        off = (pl.program_id(0) * t_new) % PAGE_SIZE
        key = lax.broadcasted_iota(jnp.int32, (rows, PAGE_SIZE), 1) - off
        pages([kn_ref], [vn_ref], (key >= 0) & (key < qpos_ref[...]))

    order = list(reversed(range(n_pages_step)))
    pages([k_refs[g] for g in order], [v_refs[g] for g in order], None)

    @pl.when(step == pl.num_programs(1) - 1)
    def _():
        for p in range(N_SLABS):
            blk = acc_ref[p * slab_rows:(p + 1) * slab_rows, :]
            o_ref[0, :, p * SLAB:(p + 1) * SLAB] = jnp.where(
                lane < HEAD_DIM, blk[:t_new], blk[t_new:]).astype(o_ref.dtype)


def _attn_decode(q, kt_new, vt_new, cache_k, cache_v, page_table, bias_rows, qpos_rows):
    bsz, t_new, d = q.shape
    assert PAGE_SIZE % t_new == 0 and kt_new.shape[2] % PAGE_SIZE == 0
    n_pages = page_table.shape[1]
    g = DECODE_PAGES_PER_STEP
    while n_pages % g:
        g //= 2
    n_steps = n_pages // g
    rows = N_HEADS * t_new

    def page_spec(slot):
        return pl.BlockSpec((1, d, PAGE_SIZE), lambda b, s, pt: (pt[b, (n_steps - 1 - s) * g + slot], 0, 0))

    per_b = lambda shape: pl.BlockSpec(shape, lambda b, s, pt: (b, 0, 0))
    new_spec = pl.BlockSpec((1, d, PAGE_SIZE), lambda b, s, pt: (0, 0, (b * t_new) // PAGE_SIZE))
    grid_spec = pltpu.PrefetchScalarGridSpec(
        num_scalar_prefetch=1,
        grid=(bsz, n_steps),
        in_specs=[per_b((1, t_new, d)), pl.BlockSpec((rows, 1), lambda b, s, pt: (0, 0)),
                  pl.BlockSpec((rows, 1), lambda b, s, pt: (0, 0)), new_spec, new_spec]
                 + [page_spec(slot) for slot in range(g)] * 2,
        out_specs=per_b((1, t_new, d)),
        scratch_shapes=[pltpu.VMEM((rows, 1), F32), pltpu.VMEM((rows, SLAB), F32)],
    )
    return pl.pallas_call(
        functools.partial(_decode_kernel, g),
        grid_spec=grid_spec,
        out_shape=jax.ShapeDtypeStruct((bsz, t_new, d), BF16),
        compiler_params=_params("parallel", "arbitrary"),
        name="attn_decode",
    )(page_table, q, bias_rows, qpos_rows, kt_new, vt_new, *([cache_k] * g), *([cache_v] * g))


def kernel(x_prompt, x_sample, cache_k, cache_v, state_pool, page_table, c_prompt, c_sample,
           w_ada, b_ada, ln_g, ln_b, w_pool, pool_scale, w_up, w_down, w_kv, w_q, w_o, b_break):
    bsz, seq, d = x_prompt.shape
    dec_b, dec_t, _ = x_sample.shape
    assert d == D_MODEL and w_ada.shape[0] == DEPTH == 2 and w_pool.shape[0] == 1 and w_q.shape[0] == 1
    past_len = page_table.shape[1] * PAGE_SIZE

    wu_bf, wd_bf = w_up.astype(BF16), w_down.astype(BF16)
    wkv_bf, wq_bf, wo_bf, wp_bf = w_kv.astype(BF16), w_q.astype(BF16), w_o.astype(BF16), w_pool.astype(BF16)
    ln_g4 = ln_g.reshape(DEPTH * 2, 1, d)
    ln_b4 = ln_b.reshape(DEPTH * 2, 1, d)

    n_tok = dec_b * dec_t
    assert n_tok % MOD_ROW_GROUP == 0
    pad_rows = lambda c: jnp.pad(c, ((0, -c.shape[0] % MOD_ROW_GROUP), (0, 0)))
    sections = [jnp.repeat(c_sample, dec_t, axis=0), pad_rows(c_prompt), pad_rows(c_sample)]
    prompt_row, seq_row = n_tok, n_tok + sections[1].shape[0]
    ada = _ada(jnp.concatenate(sections, axis=0), w_ada, b_ada)
    n_seq = MIXER_SEQS_PER_STEP
    while dec_b % n_seq:
        n_seq //= 2
    mod_p = [_ModRows(l, MOD_ROW_GROUP, lambda b: prompt_row + b, 1) for l in range(DEPTH)]
    mod_s = [_ModRows(l, n_tok, lambda b: 0, None) for l in range(DEPTH)]
    mod_seq = _ModRows(0, MOD_ROW_GROUP, lambda b: seq_row + b * n_seq, n_seq)
    xs_flat = lambda a: a.reshape(1, n_tok, d)

    zero_hist = jnp.zeros((bsz, POOL_HIST, d), F32)
    x1_p, hist_p = _mixer(x_prompt, zero_hist, ada, mod_p[0], 0, wp_bf, pool_scale, ln_g4, ln_b4, 0)
    x1_s, hist_s = _mixer(x_sample, state_pool[0], ada, mod_seq, past_len, wp_bf, pool_scale, ln_g4, ln_b4, 0)
    x2_p = _mlp(x1_p, ada, mod_p[0], wu_bf, wd_bf, ln_g4, ln_b4, 1)
    x2_s = _mlp(xs_flat(x1_s), ada, mod_s[0], wu_bf, wd_bf, ln_g4, ln_b4, 1)

    k_p, v_p, kb_p, vtb_p, q_p = _kvq(x2_p, ada, mod_p[1], wkv_bf, wq_bf, decode=False)
    k_s, v_s, kt_s, vt_s, q_s = _kvq(x2_s, ada, mod_s[1], wkv_bf, wq_bf, decode=True)

    bias = b_break[0].astype(F32)
    o_p = _attn_prompt(q_p, kb_p, vtb_p, bias)
    pages = lambda c: jnp.transpose(c, (0, 2, 3, 1)).reshape(-1, d, PAGE_SIZE)
    pad_cols = -n_tok % PAGE_SIZE
    if pad_cols:
        kt_s, vt_s = (jnp.pad(a, ((0, 0), (0, 0), (0, pad_cols))) for a in (kt_s, vt_s))
    o_s = _attn_decode(q_s.reshape(dec_b, dec_t, d), kt_s, vt_s, pages(cache_k), pages(cache_v),
                       page_table, jnp.repeat(bias, dec_t)[:, None],
                       jnp.tile(jnp.arange(dec_t, dtype=jnp.int32), N_HEADS)[:, None])

    y_p = _mlp(x2_p, ada, mod_p[1], wu_bf, wd_bf, ln_g4, ln_b4, 3, oproj=(o_p, wo_bf, 2))
    y_s = _mlp(x2_s, ada, mod_s[1], wu_bf, wd_bf, ln_g4, ln_b4, 3, oproj=(xs_flat(o_s), wo_bf, 2))

    heads = lambda a, b, t: a.reshape(b, t, N_HEADS, HEAD_DIM)
    return (y_p, y_s.reshape(dec_b, dec_t, d),
            heads(k_p, bsz, seq), heads(v_p, bsz, seq), hist_p[None],
            heads(k_s, dec_b, dec_t), heads(v_s, dec_b, dec_t), hist_s[None])
```

```python
import functools

import jax
import jax.numpy as jnp
from jax import lax
from jax.experimental import pallas as pl
from jax.experimental.pallas import tpu as pltpu

F32 = jnp.float32
BF16 = jnp.bfloat16

D_MODEL = 1024
DEPTH = 2
N_HEADS = 16
HEAD_DIM = 64
HEADS_PER_SLAB = 2
SLAB = HEADS_PER_SLAB * HEAD_DIM
N_SLABS = N_HEADS // HEADS_PER_SLAB
D_FF = 4 * D_MODEL
POOL_WINDOWS = (2, 4, 8, 16)
POOL_GROUP_DIM = D_MODEL // len(POOL_WINDOWS)
POOL_HIST = max(POOL_WINDOWS) - 1
HIST_ROWS = 2 * max(POOL_WINDOWS)
MOD_ROW_GROUP = 8
MIXER_TILE_T = 512
MIXER_SEQS_PER_STEP = 8
LN_EPS = 1e-5
DN_ALPHA = float((2 * DEPTH) ** 0.25)
ATTN_SCALE = HEAD_DIM ** -0.5
PAGE_SIZE = 128

VMEM_LIMIT_BYTES = 56 * 1024 * 1024
ADA_TILE_N = 1536
MLP_TILE_M = 512
MLP_TILE_F = 512
ATTN_BLOCK = 256
ATTN_SLABS_PER_STEP = 4
ATTN_HEADS_PER_WAVE = 8
DECODE_PAGES_PER_STEP = 16


def _params(*sem):
    return pltpu.CompilerParams(dimension_semantics=sem, vmem_limit_bytes=VMEM_LIMIT_BYTES)


def _layer_norm(y, g, b):
    mu = jnp.mean(y, axis=-1, keepdims=True)
    yc = y - mu
    var = jnp.mean(yc * yc, axis=-1, keepdims=True)
    return yc * lax.rsqrt(var + LN_EPS) * g + b


def _resident(shape, index_map):
    return pl.BlockSpec(shape, index_map, pipeline_mode=pl.Buffered(1))


def _ada_kernel(c_ref, w_ref, b_ref, o_ref):
    c = c_ref[...]
    s = c * jax.nn.sigmoid(c)
    o_ref[0] = jnp.dot(s.astype(BF16), w_ref[0].astype(BF16), preferred_element_type=F32) + b_ref[0]


def _ada(c_all, w_ada, b_ada):
    n_layers, d, n = w_ada.shape
    m = c_all.shape[0]
    return pl.pallas_call(
        _ada_kernel,
        grid=(n_layers, n // ADA_TILE_N),
        in_specs=[
            pl.BlockSpec((m, d), lambda l, j: (0, 0)),
            pl.BlockSpec((1, d, ADA_TILE_N), lambda l, j: (l, 0, j)),
            pl.BlockSpec((1, 1, ADA_TILE_N), lambda l, j: (l, 0, j)),
        ],
        out_specs=pl.BlockSpec((1, m, ADA_TILE_N), lambda l, j: (l, 0, j)),
        out_shape=jax.ShapeDtypeStruct((n_layers, m, n), F32),
        compiler_params=_params("parallel", "parallel"),
        name="ada",
    )(c_all, w_ada, b_ada.reshape(n_layers, 1, n))


class _ModRows:
    def __init__(self, layer, rows, first_row, per_step):
        self.layer, self.rows, self.first_row, self.per_step = layer, rows, first_row, per_step

    def spec(self, chunk):
        return pl.BlockSpec((1, self.rows, D_MODEL),
                            lambda b, t: (self.layer, self.first_row(b) // self.rows, chunk))

    def read(self, ref, s=0):
        if self.per_step is None:
            return ref[0]
        return ref[0, pl.ds((pl.program_id(0) * self.per_step) % self.rows + s, 1), :]


def _mixer_kernel(pos0, tile_t, n_seq, mod, x_ref, hist_ref, sh_ref, sc_ref, g_ref, wp_ref, ps_ref,
                  lng_ref, lnb_ref, o_ref, ho_ref, ext_ref, s2_ref, s4_ref, s8_ref):
    t = pl.program_id(1)
    gd = POOL_GROUP_DIM
    rows_seq = HIST_ROWS + tile_t
    n = n_seq * rows_seq

    @pl.when(t == 0)
    def _():
        for s in range(n_seq):
            ext_ref[s * rows_seq:s * rows_seq + HIST_ROWS - POOL_HIST, :] = jnp.zeros(
                (HIST_ROWS - POOL_HIST, D_MODEL), F32)
            ext_ref[s * rows_seq + HIST_ROWS - POOL_HIST:s * rows_seq + HIST_ROWS, :] = hist_ref[s]

    for s in range(n_seq):
        ext_ref[s * rows_seq + HIST_ROWS:(s + 1) * rows_seq, :] = (
            x_ref[s] * (1.0 + mod.read(sc_ref, s)) + mod.read(sh_ref, s))
    s2_ref[8:n, :] = ext_ref[8:n, gd:] + ext_ref[7:n - 1, gd:]
    s4_ref[16:n, :] = s2_ref[16:n, :] + s2_ref[14:n - 2, :]
    s8_ref[24:n, :] = s4_ref[24:n, gd:] + s4_ref[20:n - 4, gd:]

    pos = pos0 + t * tile_t + lax.broadcasted_iota(jnp.int32, (tile_t, 1), 0)
    inv_cnt = [1.0 / jnp.minimum(w, pos + 1).astype(F32) for w in POOL_WINDOWS]
    ds = [[] for _ in POOL_WINDOWS]
    for s in range(n_seq):
        a, b = s * rows_seq + HIST_ROWS, (s + 1) * rows_seq
        wsums = [ext_ref[a:b, :gd] + ext_ref[a - 1:b - 1, :gd], s4_ref[a:b, :gd], s8_ref[a:b, :gd],
                 s8_ref[a:b, gd:] + s8_ref[a - 8:b - 8, gd:]]
        for g, wsum in enumerate(wsums):
            ds[g].append(wsum * inv_cnt[g] - ext_ref[a:b, g * gd:(g + 1) * gd])
    mix = jnp.concatenate(
        [jnp.dot(jnp.concatenate(ds[g], axis=0).astype(BF16), wp_ref[0, g], preferred_element_type=F32)
         for g in range(len(POOL_WINDOWS))], axis=-1) * ps_ref[...]
    for s in range(n_seq):
        y = DN_ALPHA * x_ref[s] + mod.read(g_ref, s) * mix[s * tile_t:(s + 1) * tile_t]
        o_ref[s] = _layer_norm(y, lng_ref[0], lnb_ref[0])
        ho_ref[s] = ext_ref[(s + 1) * rows_seq - POOL_HIST:(s + 1) * rows_seq, :]
        ext_ref[s * rows_seq:s * rows_seq + HIST_ROWS, :] = ext_ref[(s + 1) * rows_seq - HIST_ROWS:
                                                                    (s + 1) * rows_seq, :]


def _mixer(x, hist, ada, mod, pos0, wp_bf, pool_scale, ln_g, ln_b, ln_idx):
    bsz, seq, d = x.shape
    tile_t = min(seq, MIXER_TILE_T)
    n_seq = mod.per_step
    assert bsz % n_seq == 0 and tile_t % 8 == 0
    n = n_seq * (HIST_ROWS + tile_t)
    gd = POOL_GROUP_DIM
    return pl.pallas_call(
        functools.partial(_mixer_kernel, pos0, tile_t, n_seq, mod),
        grid=(bsz // n_seq, seq // tile_t),
        in_specs=[
            pl.BlockSpec((n_seq, tile_t, d), lambda b, t: (b, t, 0)),
            pl.BlockSpec((n_seq, POOL_HIST, d), lambda b, t: (b, 0, 0)),
            mod.spec(0), mod.spec(1), mod.spec(2),
            pl.BlockSpec(wp_bf.shape, lambda b, t: (0, 0, 0, 0)),
            pl.BlockSpec((1, d), lambda b, t: (0, 0)),
            pl.BlockSpec((1, 1, d), lambda b, t: (ln_idx, 0, 0)),
            pl.BlockSpec((1, 1, d), lambda b, t: (ln_idx, 0, 0)),
        ],
        out_specs=[
            pl.BlockSpec((n_seq, tile_t, d), lambda b, t: (b, t, 0)),
            pl.BlockSpec((n_seq, POOL_HIST, d), lambda b, t: (b, 0, 0)),
        ],
        out_shape=[jax.ShapeDtypeStruct((bsz, seq, d), F32),
                   jax.ShapeDtypeStruct((bsz, POOL_HIST, d), F32)],
        scratch_shapes=[pltpu.VMEM((n, d), F32), pltpu.VMEM((n, d - gd), F32),
                        pltpu.VMEM((n, d - gd), F32), pltpu.VMEM((n, d - 2 * gd), F32)],
        compiler_params=_params("parallel", "arbitrary"),
        name="mixer",
    )(x, hist, ada, ada, ada, wp_bf, pool_scale, ln_g, ln_b)


def _mlp_kernel(has_oproj, mod, *refs):
    if has_oproj:
        (x_ref, o_ref, g1_ref, wo_ref, lng1_ref, lnb1_ref,
         sh_ref, sc_ref, g_ref, wu_ref, wd_ref, lng_ref, lnb_ref, out_ref, acc_ref) = refs
        mix = jnp.dot(o_ref[0], wo_ref[0], preferred_element_type=F32)
        x = _layer_norm(DN_ALPHA * x_ref[0] + mod.read(g1_ref) * mix, lng1_ref[0], lnb1_ref[0])
    else:
        (x_ref, sh_ref, sc_ref, g_ref, wu_ref, wd_ref, lng_ref, lnb_ref, out_ref, acc_ref) = refs
        x = x_ref[0]
    h = (x * (1.0 + mod.read(sc_ref)) + mod.read(sh_ref)).astype(BF16)
    for c in range(D_FF // MLP_TILE_F):
        lo, hi = c * MLP_TILE_F, (c + 1) * MLP_TILE_F
        u = jnp.dot(h, wu_ref[0, :, lo:hi], preferred_element_type=F32)
        u = jnp.maximum(u, 0.0)
        contrib = jnp.dot((u * u).astype(BF16), wd_ref[0, lo:hi, :], preferred_element_type=F32)
        if c == 0:
            acc_ref[...] = contrib
        else:
            acc_ref[...] += contrib
    y = DN_ALPHA * x + mod.read(g_ref) * acc_ref[...]
    out_ref[0] = _layer_norm(y, lng_ref[0], lnb_ref[0])


def _mlp(x, ada, mod, wu_bf, wd_bf, ln_g, ln_b, ln_idx, oproj=None):
    grp, rows, d = x.shape
    tile_m = min(rows, MLP_TILE_M)
    xspec = pl.BlockSpec((1, tile_m, d), lambda b, t: (b, t, 0))
    lnspec = lambda idx: pl.BlockSpec((1, 1, d), lambda b, t: (idx, 0, 0))
    ins, specs = [x], [xspec]
    if oproj is not None:
        o_bf, wo_bf, ln_idx1 = oproj
        ins += [o_bf, ada, wo_bf, ln_g, ln_b]
        specs += [xspec, mod.spec(2), _resident((1, d, d), lambda b, t: (0, 0, 0)),
                  lnspec(ln_idx1), lnspec(ln_idx1)]
    ins += [ada, ada, ada, wu_bf, wd_bf, ln_g, ln_b]
    specs += [mod.spec(3), mod.spec(4), mod.spec(5),
              _resident((1, d, D_FF), lambda b, t: (mod.layer, 0, 0)),
              _resident((1, D_FF, d), lambda b, t: (mod.layer, 0, 0)),
              lnspec(ln_idx), lnspec(ln_idx)]
    return pl.pallas_call(
        functools.partial(_mlp_kernel, oproj is not None, mod),
        grid=(grp, rows // tile_m),
        in_specs=specs,
        out_specs=xspec,
        out_shape=jax.ShapeDtypeStruct((grp, rows, d), F32),
        scratch_shapes=[pltpu.VMEM((tile_m, d), F32)],
        compiler_params=_params("parallel", "parallel"),
        name="mlp_oproj" if oproj is not None else "mlp",
    )(*ins)


def _kvq_kernel(decode, mod, x_ref, sh_ref, sc_ref, wkv_ref, wq_ref, k_ref, v_ref, k2_ref, v2_ref, q_ref):
    x = x_ref[0]
    kv = jnp.dot(x.astype(BF16), wkv_ref[...], preferred_element_type=F32)
    k, v = kv[:, :D_MODEL], kv[:, D_MODEL:]
    k_ref[0] = k
    v_ref[0] = v
    if decode:
        k2_ref[0] = k.T
        v2_ref[0] = v.T
    else:
        k2_ref[0] = k.astype(BF16)
        blk = v2_ref.shape[3]
        for kb in range(v2_ref.shape[1]):
            v2_ref[0, kb] = v[kb * blk:(kb + 1) * blk, :].T.astype(BF16)
    h = (x * (1.0 + mod.read(sc_ref)) + mod.read(sh_ref)).astype(BF16)
    q_ref[0] = (jnp.dot(h, wq_ref[0], preferred_element_type=F32) * (ATTN_SCALE * LOG2E)).astype(q_ref.dtype)


def _kvq(x, ada, mod, wkv_bf, wq_bf, decode):
    grp, rows, d = x.shape
    tile_m = min(rows, MLP_TILE_M)
    xspec = pl.BlockSpec((1, tile_m, d), lambda b, t: (b, t, 0))
    sds = lambda dt: jax.ShapeDtypeStruct((grp, rows, d), dt)
    if decode:
        tspec = pl.BlockSpec((1, d, tile_m), lambda b, t: (b, 0, t))
        specs2, sds2 = [tspec, tspec], [jax.ShapeDtypeStruct((grp, d, rows), F32)] * 2
    else:
        blk = min(rows, ATTN_BLOCK)
        specs2 = [xspec, pl.BlockSpec((1, tile_m // blk, d, blk), lambda b, t: (b, t, 0, 0))]
        sds2 = [sds(BF16), jax.ShapeDtypeStruct((grp, rows // blk, d, blk), BF16)]
    return pl.pallas_call(
        functools.partial(_kvq_kernel, decode, mod),
        grid=(grp, rows // tile_m),
        in_specs=[xspec, mod.spec(0), mod.spec(1),
                  _resident((d, 2 * d), lambda b, t: (0, 0)), _resident((1, d, d), lambda b, t: (0, 0, 0))],
        out_specs=[xspec, xspec] + specs2 + [xspec],
        out_shape=[sds(F32), sds(F32)] + sds2 + [sds(F32 if decode else BF16)],
        compiler_params=_params("parallel", "parallel"),
        name="kvq",
    )(x, ada, ada, wkv_bf, wq_bf)


LOG2E = 1.4426950408889634
SOFTPLUS_LINEAR_FROM = 30.0 * LOG2E


def _softplus2(z2):
    return jnp.maximum(jnp.log(1.0 + jnp.exp2(jnp.minimum(z2, SOFTPLUS_LINEAR_FROM))) * LOG2E, z2)


def _triangle(n, transposed=False):
    r = lax.broadcasted_iota(jnp.int32, (n, n), 0)
    c = lax.broadcasted_iota(jnp.int32, (n, n), 1)
    return jnp.where((c > r) if transposed else (r > c), 1.0, 0.0).astype(BF16)


def _attn_kernel(blk, bias_ref, q_ref, k_ref, vt_ref, o_ref):
    group = pl.program_id(1)
    i = pl.program_id(2)
    lane = lax.broadcasted_iota(jnp.int32, (1, SLAB), 1)
    key = lax.broadcasted_iota(jnp.int32, (blk, blk), 0)
    qry = lax.broadcasted_iota(jnp.int32, (blk, blk), 1)
    causal = key < qry
    tri = _triangle(blk, transposed=True)
    heads = [(sl, hh) for sl in range(ATTN_SLABS_PER_STEP) for hh in range(HEADS_PER_SLAB)]
    qms, biases = [], []
    for sl, hh in heads:
        q = q_ref[0, :, sl * SLAB:(sl + 1) * SLAB]
        in_head = (lane >= hh * HEAD_DIM) & (lane < (hh + 1) * HEAD_DIM)
        qms.append(jnp.where(in_head, q, jnp.zeros_like(q)))
        biases.append(bias_ref[(group * ATTN_SLABS_PER_STEP + sl) * HEADS_PER_SLAB + hh])

    def tile(j, carry, mask):
        start = pl.multiple_of(j * blk, blk)
        new = []
        for w0 in range(0, len(heads), ATTN_HEADS_PER_WAVE):
            wave = list(range(w0, min(w0 + ATTN_HEADS_PER_WAVE, len(heads))))
            zs = [lax.dot_general(k_ref[0, pl.ds(start, blk), heads[c][0] * SLAB:(heads[c][0] + 1) * SLAB],
                                  qms[c], (((1,), (1,)), ((), ())), preferred_element_type=F32) + biases[c]
                  for c in wave]
            nks = [_softplus2(z) for z in zs]
            if mask is not None:
                nks = [jnp.where(mask, nk, 0.0) for nk in nks]
            log_betas = [z - nk for z, nk in zip(zs, nks)]
            first = [nk[0:1, :] for nk in nks]
            excls = [jnp.dot(tri, nk.astype(BF16), preferred_element_type=F32) for nk in nks]
            ws = [jnp.exp2(lb - excl - carry[c][0]) for c, lb, excl in zip(wave, log_betas, excls)]
            if mask is not None:
                ws = [jnp.where(mask, a, 0.0) for a in ws]
            for n, c in enumerate(wave):
                lo = heads[c][0] * SLAB + heads[c][1] * HEAD_DIM
                vt = vt_ref[0, j, lo:lo + HEAD_DIM, :]
                new.append((carry[c][0] + (excls[n][0:1, :] + first[n]),
                            carry[c][1] + jnp.dot(vt, ws[n].astype(BF16), preferred_element_type=F32)))
        return tuple(new)

    init = tuple((jnp.zeros((1, blk), F32), jnp.zeros((HEAD_DIM, blk), F32)) for _ in heads)
    carry = tile(i, init, causal)
    carry = lax.fori_loop(0, i, lambda n, c: tile(i - 1 - n, c, None), carry)
    for sl in range(ATTN_SLABS_PER_STEP):
        ot = jnp.concatenate([carry[HEADS_PER_SLAB * sl + hh][1] for hh in range(HEADS_PER_SLAB)], axis=0)
        o_ref[0, :, sl * SLAB:(sl + 1) * SLAB] = ot.T.astype(o_ref.dtype)


def _attn_prompt(q_bf, k_bf, vt_bf, bias):
    bsz, seq, d = q_bf.shape
    blk = vt_bf.shape[3]
    width = ATTN_SLABS_PER_STEP * SLAB
    qspec = pl.BlockSpec((1, blk, width), lambda b, p, i: (b, i, p))
    kspec = pl.BlockSpec((1, seq, width), lambda b, p, i: (b, 0, p))
    vspec = pl.BlockSpec((1, seq // blk, width, blk), lambda b, p, i: (b, 0, p, 0))
    return pl.pallas_call(
        functools.partial(_attn_kernel, blk),
        grid=(bsz, N_SLABS // ATTN_SLABS_PER_STEP, seq // blk),
        in_specs=[pl.BlockSpec(memory_space=pltpu.SMEM), qspec, kspec, vspec],
        out_specs=qspec,
        out_shape=jax.ShapeDtypeStruct((bsz, seq, d), BF16),
        compiler_params=_params("parallel", "parallel", "arbitrary"),
        name="attn_prompt",
    )(bias, q_bf, k_bf, vt_bf)


def _decode_kernel(n_pages_step, *refs):
    pt_ref = refs[0]
    q_ref, bias_ref, qpos_ref, kn_ref, vn_ref = refs[1:6]
    k_refs = refs[6:6 + n_pages_step]
    v_refs = refs[6 + n_pages_step:6 + 2 * n_pages_step]
    o_ref, r_ref, acc_ref = refs[6 + 2 * n_pages_step:]
    del pt_ref
    step = pl.program_id(1)
    t_new = q_ref.shape[1]
    rows = N_SLABS * HEADS_PER_SLAB * t_new
    slab_rows = HEADS_PER_SLAB * t_new

    q = q_ref[0]
    lane = lax.broadcasted_iota(jnp.int32, (1, SLAB), 1)
    qms = []
    for p in range(N_SLABS):
        qp = q[:, p * SLAB:(p + 1) * SLAB]
        parts = [jnp.where((lane >= hh * HEAD_DIM) & (lane < (hh + 1) * HEAD_DIM), qp, 0.0)
                 for hh in range(HEADS_PER_SLAB)]
        qms.append(jnp.concatenate(parts, axis=0).astype(BF16))
    bias = bias_ref[...]
    tri = _triangle(PAGE_SIZE)

    def pages(kts, vts, mask):
        zs = [jnp.concatenate(
            [jnp.dot(qms[p], kt[0, p * SLAB:(p + 1) * SLAB, :].astype(BF16), preferred_element_type=F32)
             for p in range(N_SLABS)], axis=0) + bias for kt in kts]
        nks = [_softplus2(z) for z in zs]
        if mask is not None:
            nks = [jnp.where(mask, nk, 0.0) for nk in nks]
        excls = [jnp.dot(nk.astype(BF16), tri, preferred_element_type=F32) for nk in nks]
        r = r_ref[...]
        ws = []
        for z, nk, excl in zip(zs, nks, excls):
            a = jnp.exp2((z - nk) - excl - r)
            if mask is not None:
                a = jnp.where(mask, a, 0.0)
            ws.append(a.astype(BF16))
            r = r + (excl[:, 0:1] + nk[:, 0:1])
        r_ref[...] = r
        for p in range(N_SLABS):
            part = acc_ref[p * slab_rows:(p + 1) * slab_rows, :]
            for w, vt in zip(ws, vts):
                part = part + jnp.dot(
                    w[p * slab_rows:(p + 1) * slab_rows, :], vt[0, p * SLAB:(p + 1) * SLAB, :].T.astype(BF16),
                    preferred_element_type=F32)
            acc_ref[p * slab_rows:(p + 1) * slab_rows, :] = part

    @pl.when(step == 0)
    def _():
        r_ref[...] = jnp.zeros_like(r_ref)
        acc_ref[...] = jnp.zeros_like(acc_ref)
        off = (pl.program_id(0) * t_new) % PAGE_SIZE
        key = lax.broadcasted_iota(jnp.int32, (rows, PAGE_SIZE), 1) - off
        pages([kn_ref], [vn_ref], (key >= 0) & (key < qpos_ref[...]))

    order = list(reversed(range(n_pages_step)))
    pages([k_refs[g] for g in order], [v_refs[g] for g in order], None)

    @pl.when(step == pl.num_programs(1) - 1)
    def _():
        for p in range(N_SLABS):
            blk = acc_ref[p * slab_rows:(p + 1) * slab_rows, :]
            o_ref[0, :, p * SLAB:(p + 1) * SLAB] = jnp.where(
                lane < HEAD_DIM, blk[:t_new], blk[t_new:]).astype(o_ref.dtype)


def _attn_decode(q, kt_new, vt_new, cache_k, cache_v, page_table, bias_rows, qpos_rows):
    bsz, t_new, d = q.shape
    assert PAGE_SIZE % t_new == 0 and kt_new.shape[2] % PAGE_SIZE == 0
    n_pages = page_table.shape[1]
    g = DECODE_PAGES_PER_STEP
    while n_pages % g:
        g //= 2
    n_steps = n_pages // g
    rows = N_HEADS * t_new

    def page_spec(slot):
        return pl.BlockSpec((1, d, PAGE_SIZE), lambda b, s, pt: (pt[b, (n_steps - 1 - s) * g + slot], 0, 0))

    per_b = lambda shape: pl.BlockSpec(shape, lambda b, s, pt: (b, 0, 0))
    new_spec = pl.BlockSpec((1, d, PAGE_SIZE), lambda b, s, pt: (0, 0, (b * t_new) // PAGE_SIZE))
    grid_spec = pltpu.PrefetchScalarGridSpec(
        num_scalar_prefetch=1,
        grid=(bsz, n_steps),
        in_specs=[per_b((1, t_new, d)), pl.BlockSpec((rows, 1), lambda b, s, pt: (0, 0)),
                  pl.BlockSpec((rows, 1), lambda b, s, pt: (0, 0)), new_spec, new_spec]
                 + [page_spec(slot) for slot in range(g)] * 2,
        out_specs=per_b((1, t_new, d)),
        scratch_shapes=[pltpu.VMEM((rows, 1), F32), pltpu.VMEM((rows, SLAB), F32)],
    )
    return pl.pallas_call(
        functools.partial(_decode_kernel, g),
        grid_spec=grid_spec,
        out_shape=jax.ShapeDtypeStruct((bsz, t_new, d), BF16),
        compiler_params=_params("parallel", "arbitrary"),
        name="attn_decode",
    )(page_table, q, bias_rows, qpos_rows, kt_new, vt_new, *([cache_k] * g), *([cache_v] * g))


def kernel(x_prompt, x_sample, cache_k, cache_v, state_pool, page_table, c_prompt, c_sample,
           w_ada, b_ada, ln_g, ln_b, w_pool, pool_scale, w_up, w_down, w_kv, w_q, w_o, b_break):
    bsz, seq, d = x_prompt.shape
    dec_b, dec_t, _ = x_sample.shape
    assert d == D_MODEL and w_ada.shape[0] == DEPTH == 2 and w_pool.shape[0] == 1 and w_q.shape[0] == 1
    past_len = page_table.shape[1] * PAGE_SIZE

    wu_bf, wd_bf = w_up.astype(BF16), w_down.astype(BF16)
    wkv_bf, wq_bf, wo_bf, wp_bf = w_kv.astype(BF16), w_q.astype(BF16), w_o.astype(BF16), w_pool.astype(BF16)
    ln_g4 = ln_g.reshape(DEPTH * 2, 1, d)
    ln_b4 = ln_b.reshape(DEPTH * 2, 1, d)

    n_tok = dec_b * dec_t
    assert n_tok % MOD_ROW_GROUP == 0
    pad_rows = lambda c: jnp.pad(c, ((0, -c.shape[0] % MOD_ROW_GROUP), (0, 0)))
    sections = [jnp.repeat(c_sample, dec_t, axis=0), pad_rows(c_prompt), pad_rows(c_sample)]
    prompt_row, seq_row = n_tok, n_tok + sections[1].shape[0]
    ada = _ada(jnp.concatenate(sections, axis=0), w_ada, b_ada)
    n_seq = MIXER_SEQS_PER_STEP
    while dec_b % n_seq:
        n_seq //= 2
    mod_p = [_ModRows(l, MOD_ROW_GROUP, lambda b: prompt_row + b, 1) for l in range(DEPTH)]
    mod_s = [_ModRows(l, n_tok, lambda b: 0, None) for l in range(DEPTH)]
    mod_seq = _ModRows(0, MOD_ROW_GROUP, lambda b: seq_row + b * n_seq, n_seq)
    xs_flat = lambda a: a.reshape(1, n_tok, d)

    zero_hist = jnp.zeros((bsz, POOL_HIST, d), F32)
    x1_p, hist_p = _mixer(x_prompt, zero_hist, ada, mod_p[0], 0, wp_bf, pool_scale, ln_g4, ln_b4, 0)
    x1_s, hist_s = _mixer(x_sample, state_pool[0], ada, mod_seq, past_len, wp_bf, pool_scale, ln_g4, ln_b4, 0)
    x2_p = _mlp(x1_p, ada, mod_p[0], wu_bf, wd_bf, ln_g4, ln_b4, 1)
    x2_s = _mlp(xs_flat(x1_s), ada, mod_s[0], wu_bf, wd_bf, ln_g4, ln_b4, 1)

    k_p, v_p, kb_p, vtb_p, q_p = _kvq(x2_p, ada, mod_p[1], wkv_bf, wq_bf, decode=False)
    k_s, v_s, kt_s, vt_s, q_s = _kvq(x2_s, ada, mod_s[1], wkv_bf, wq_bf, decode=True)

    bias = b_break[0].astype(F32) * LOG2E
    o_p = _attn_prompt(q_p, kb_p, vtb_p, bias)
    pages = lambda c: jnp.transpose(c, (0, 2, 3, 1)).reshape(-1, d, PAGE_SIZE)
    pad_cols = -n_tok % PAGE_SIZE
    if pad_cols:
        kt_s, vt_s = (jnp.pad(a, ((0, 0), (0, 0), (0, pad_cols))) for a in (kt_s, vt_s))
    o_s = _attn_decode(q_s.reshape(dec_b, dec_t, d), kt_s, vt_s, pages(cache_k), pages(cache_v),
                       page_table, jnp.repeat(bias, dec_t)[:, None],
                       jnp.tile(jnp.arange(dec_t, dtype=jnp.int32), N_HEADS)[:, None])

    y_p = _mlp(x2_p, ada, mod_p[1], wu_bf, wd_bf, ln_g4, ln_b4, 3, oproj=(o_p, wo_bf, 2))
    y_s = _mlp(x2_s, ada, mod_s[1], wu_bf, wd_bf, ln_g4, ln_b4, 3, oproj=(xs_flat(o_s), wo_bf, 2))

    heads = lambda a, b, t: a.reshape(b, t, N_HEADS, HEAD_DIM)
    return (y_p, y_s.reshape(dec_b, dec_t, d),
            heads(k_p, bsz, seq), heads(v_p, bsz, seq), hist_p[None],
            heads(k_s, dec_b, dec_t), heads(v_s, dec_b, dec_t), hist_s[None])
```

```python
import functools

import jax
import jax.numpy as jnp
from jax import lax
from jax.experimental import pallas as pl
from jax.experimental.pallas import tpu as pltpu

F32 = jnp.float32
BF16 = jnp.bfloat16

D_MODEL = 1024
DEPTH = 2
N_HEADS = 16
HEAD_DIM = 64
HEADS_PER_SLAB = 2
SLAB = HEADS_PER_SLAB * HEAD_DIM
N_SLABS = N_HEADS // HEADS_PER_SLAB
D_FF = 4 * D_MODEL
POOL_WINDOWS = (2, 4, 8, 16)
POOL_GROUP_DIM = D_MODEL // len(POOL_WINDOWS)
POOL_HIST = max(POOL_WINDOWS) - 1
HIST_ROWS = 2 * max(POOL_WINDOWS)
MOD_ROW_GROUP = 8
MIXER_TILE_T = 1024
MIXER_SEQS_PER_STEP = 8
LN_EPS = 1e-5
DN_ALPHA = float((2 * DEPTH) ** 0.25)
ATTN_SCALE = HEAD_DIM ** -0.5
PAGE_SIZE = 128

VMEM_LIMIT_BYTES = 56 * 1024 * 1024
ADA_TILE_N = 1536
MLP_TILE_M = 512
MLP_TILE_F = 512
ATTN_BLOCK = 256
ATTN_SLABS_PER_STEP = 4
ATTN_HEADS_PER_WAVE = 8
DECODE_PAGES_PER_STEP = 16


def _params(*sem):
    return pltpu.CompilerParams(dimension_semantics=sem, vmem_limit_bytes=VMEM_LIMIT_BYTES)


def _layer_norm(y, g, b):
    mu = jnp.mean(y, axis=-1, keepdims=True)
    yc = y - mu
    var = jnp.mean(yc * yc, axis=-1, keepdims=True)
    return yc * lax.rsqrt(var + LN_EPS) * g + b


def _resident(shape, index_map):
    return pl.BlockSpec(shape, index_map, pipeline_mode=pl.Buffered(1))


def _ada_kernel(c_ref, w_ref, b_ref, o_ref):
    c = c_ref[...]
    s = c * jax.nn.sigmoid(c)
    o_ref[0] = jnp.dot(s.astype(BF16), w_ref[0].astype(BF16), preferred_element_type=F32) + b_ref[0]


def _ada(c_all, w_ada, b_ada):
    n_layers, d, n = w_ada.shape
    m = c_all.shape[0]
    return pl.pallas_call(
        _ada_kernel,
        grid=(n_layers, n // ADA_TILE_N),
        in_specs=[
            pl.BlockSpec((m, d), lambda l, j: (0, 0)),
            pl.BlockSpec((1, d, ADA_TILE_N), lambda l, j: (l, 0, j)),
            pl.BlockSpec((1, 1, ADA_TILE_N), lambda l, j: (l, 0, j)),
        ],
        out_specs=pl.BlockSpec((1, m, ADA_TILE_N), lambda l, j: (l, 0, j)),
        out_shape=jax.ShapeDtypeStruct((n_layers, m, n), F32),
        compiler_params=_params("parallel", "parallel"),
        name="ada",
    )(c_all, w_ada, b_ada.reshape(n_layers, 1, n))


class _ModRows:
    def __init__(self, layer, rows, first_row, per_step):
        self.layer, self.rows, self.first_row, self.per_step = layer, rows, first_row, per_step

    def spec(self, chunk):
        return pl.BlockSpec((1, self.rows, D_MODEL),
                            lambda b, t: (self.layer, self.first_row(b) // self.rows, chunk))

    def read(self, ref, s=0):
        if self.per_step is None:
            return ref[0]
        return ref[0, pl.ds((pl.program_id(0) * self.per_step) % self.rows + s, 1), :]


def _mixer_kernel(pos0, tile_t, n_seq, mod, x_ref, hist_ref, sh_ref, sc_ref, g_ref, wp_ref, ps_ref,
                  lng_ref, lnb_ref, o_ref, ho_ref, ext_ref, s2_ref, s4_ref, s8_ref):
    t = pl.program_id(1)
    gd = POOL_GROUP_DIM
    rows_seq = HIST_ROWS + tile_t
    n = n_seq * rows_seq

    @pl.when(t == 0)
    def _():
        for s in range(n_seq):
            ext_ref[s * rows_seq:s * rows_seq + HIST_ROWS - POOL_HIST, :] = jnp.zeros(
                (HIST_ROWS - POOL_HIST, D_MODEL), F32)
            ext_ref[s * rows_seq + HIST_ROWS - POOL_HIST:s * rows_seq + HIST_ROWS, :] = hist_ref[s]

    for s in range(n_seq):
        ext_ref[s * rows_seq + HIST_ROWS:(s + 1) * rows_seq, :] = (
            x_ref[s] * (1.0 + mod.read(sc_ref, s)) + mod.read(sh_ref, s))
    s2_ref[8:n, :] = ext_ref[8:n, gd:] + ext_ref[7:n - 1, gd:]
    s4_ref[16:n, :] = s2_ref[16:n, :] + s2_ref[14:n - 2, :]
    s8_ref[24:n, :] = s4_ref[24:n, gd:] + s4_ref[20:n - 4, gd:]

    pos = pos0 + t * tile_t + lax.broadcasted_iota(jnp.int32, (tile_t, 1), 0)
    inv_cnt = [1.0 / jnp.minimum(w, pos + 1).astype(F32) for w in POOL_WINDOWS]
    ds = [[] for _ in POOL_WINDOWS]
    for s in range(n_seq):
        a, b = s * rows_seq + HIST_ROWS, (s + 1) * rows_seq
        wsums = [ext_ref[a:b, :gd] + ext_ref[a - 1:b - 1, :gd], s4_ref[a:b, :gd], s8_ref[a:b, :gd],
                 s8_ref[a:b, gd:] + s8_ref[a - 8:b - 8, gd:]]
        for g, wsum in enumerate(wsums):
            ds[g].append(wsum * inv_cnt[g] - ext_ref[a:b, g * gd:(g + 1) * gd])
    mix = jnp.concatenate(
        [jnp.dot(jnp.concatenate(ds[g], axis=0).astype(BF16), wp_ref[0, g], preferred_element_type=F32)
         for g in range(len(POOL_WINDOWS))], axis=-1) * ps_ref[...]
    for s in range(n_seq):
        y = DN_ALPHA * x_ref[s] + mod.read(g_ref, s) * mix[s * tile_t:(s + 1) * tile_t]
        o_ref[s] = _layer_norm(y, lng_ref[0], lnb_ref[0])
        ho_ref[s] = ext_ref[(s + 1) * rows_seq - POOL_HIST:(s + 1) * rows_seq, :]
        ext_ref[s * rows_seq:s * rows_seq + HIST_ROWS, :] = ext_ref[(s + 1) * rows_seq - HIST_ROWS:
                                                                    (s + 1) * rows_seq, :]


def _mixer(x, hist, ada, mod, pos0, wp_bf, pool_scale, ln_g, ln_b, ln_idx):
    bsz, seq, d = x.shape
    tile_t = min(seq, MIXER_TILE_T)
    n_seq = mod.per_step
    assert bsz % n_seq == 0 and tile_t % 8 == 0
    n = n_seq * (HIST_ROWS + tile_t)
    gd = POOL_GROUP_DIM
    return pl.pallas_call(
        functools.partial(_mixer_kernel, pos0, tile_t, n_seq, mod),
        grid=(bsz // n_seq, seq // tile_t),
        in_specs=[
            pl.BlockSpec((n_seq, tile_t, d), lambda b, t: (b, t, 0)),
            pl.BlockSpec((n_seq, POOL_HIST, d), lambda b, t: (b, 0, 0)),
            mod.spec(0), mod.spec(1), mod.spec(2),
            pl.BlockSpec(wp_bf.shape, lambda b, t: (0, 0, 0, 0)),
            pl.BlockSpec((1, d), lambda b, t: (0, 0)),
            pl.BlockSpec((1, 1, d), lambda b, t: (ln_idx, 0, 0)),
            pl.BlockSpec((1, 1, d), lambda b, t: (ln_idx, 0, 0)),
        ],
        out_specs=[
            pl.BlockSpec((n_seq, tile_t, d), lambda b, t: (b, t, 0)),
            pl.BlockSpec((n_seq, POOL_HIST, d), lambda b, t: (b, 0, 0)),
        ],
        out_shape=[jax.ShapeDtypeStruct((bsz, seq, d), F32),
                   jax.ShapeDtypeStruct((bsz, POOL_HIST, d), F32)],
        scratch_shapes=[pltpu.VMEM((n, d), F32), pltpu.VMEM((n, d - gd), F32),
                        pltpu.VMEM((n, d - gd), F32), pltpu.VMEM((n, d - 2 * gd), F32)],
        compiler_params=_params("parallel", "arbitrary"),
        name="mixer",
    )(x, hist, ada, ada, ada, wp_bf, pool_scale, ln_g, ln_b)


def _mlp_kernel(has_oproj, mod, *refs):
    if has_oproj:
        (x_ref, o_ref, g1_ref, wo_ref, lng1_ref, lnb1_ref,
         sh_ref, sc_ref, g_ref, wu_ref, wd_ref, lng_ref, lnb_ref, out_ref, acc_ref) = refs
        mix = jnp.dot(o_ref[0], wo_ref[0], preferred_element_type=F32)
        x = _layer_norm(DN_ALPHA * x_ref[0] + mod.read(g1_ref) * mix, lng1_ref[0], lnb1_ref[0])
    else:
        (x_ref, sh_ref, sc_ref, g_ref, wu_ref, wd_ref, lng_ref, lnb_ref, out_ref, acc_ref) = refs
        x = x_ref[0]
    h = (x * (1.0 + mod.read(sc_ref)) + mod.read(sh_ref)).astype(BF16)
    for c in range(D_FF // MLP_TILE_F):
        lo, hi = c * MLP_TILE_F, (c + 1) * MLP_TILE_F
        u = jnp.dot(h, wu_ref[0, :, lo:hi], preferred_element_type=F32)
        u = jnp.maximum(u, 0.0)
        contrib = jnp.dot((u * u).astype(BF16), wd_ref[0, lo:hi, :], preferred_element_type=F32)
        if c == 0:
            acc_ref[...] = contrib
        else:
            acc_ref[...] += contrib
    y = DN_ALPHA * x + mod.read(g_ref) * acc_ref[...]
    out_ref[0] = _layer_norm(y, lng_ref[0], lnb_ref[0])


def _mlp(x, ada, mod, wu_bf, wd_bf, ln_g, ln_b, ln_idx, oproj=None):
    grp, rows, d = x.shape
    tile_m = min(rows, MLP_TILE_M)
    xspec = pl.BlockSpec((1, tile_m, d), lambda b, t: (b, t, 0))
    lnspec = lambda idx: pl.BlockSpec((1, 1, d), lambda b, t: (idx, 0, 0))
    ins, specs = [x], [xspec]
    if oproj is not None:
        o_bf, wo_bf, ln_idx1 = oproj
        ins += [o_bf, ada, wo_bf, ln_g, ln_b]
        specs += [xspec, mod.spec(2), _resident((1, d, d), lambda b, t: (0, 0, 0)),
                  lnspec(ln_idx1), lnspec(ln_idx1)]
    ins += [ada, ada, ada, wu_bf, wd_bf, ln_g, ln_b]
    specs += [mod.spec(3), mod.spec(4), mod.spec(5),
              _resident((1, d, D_FF), lambda b, t: (mod.layer, 0, 0)),
              _resident((1, D_FF, d), lambda b, t: (mod.layer, 0, 0)),
              lnspec(ln_idx), lnspec(ln_idx)]
    return pl.pallas_call(
        functools.partial(_mlp_kernel, oproj is not None, mod),
        grid=(grp, rows // tile_m),
        in_specs=specs,
        out_specs=xspec,
        out_shape=jax.ShapeDtypeStruct((grp, rows, d), F32),
        scratch_shapes=[pltpu.VMEM((tile_m, d), F32)],
        compiler_params=_params("parallel", "parallel"),
        name="mlp_oproj" if oproj is not None else "mlp",
    )(*ins)


def _kvq_kernel(decode, mod, x_ref, sh_ref, sc_ref, wkv_ref, wq_ref, k_ref, v_ref, k2_ref, v2_ref, q_ref):
    x = x_ref[0]
    kv = jnp.dot(x.astype(BF16), wkv_ref[...], preferred_element_type=F32)
    k, v = kv[:, :D_MODEL], kv[:, D_MODEL:]
    k_ref[0] = k
    v_ref[0] = v
    if decode:
        k2_ref[0] = k.T
        v2_ref[0] = v.T
    else:
        k2_ref[0] = k.astype(BF16)
        blk = v2_ref.shape[3]
        for kb in range(v2_ref.shape[1]):
            v2_ref[0, kb] = v[kb * blk:(kb + 1) * blk, :].T.astype(BF16)
    h = (x * (1.0 + mod.read(sc_ref)) + mod.read(sh_ref)).astype(BF16)
    q_ref[0] = (jnp.dot(h, wq_ref[0], preferred_element_type=F32) * ATTN_SCALE).astype(q_ref.dtype)


def _kvq(x, ada, mod, wkv_bf, wq_bf, decode):
    grp, rows, d = x.shape
    tile_m = min(rows, MLP_TILE_M)
    xspec = pl.BlockSpec((1, tile_m, d), lambda b, t: (b, t, 0))
    sds = lambda dt: jax.ShapeDtypeStruct((grp, rows, d), dt)
    if decode:
        tspec = pl.BlockSpec((1, d, tile_m), lambda b, t: (b, 0, t))
        specs2, sds2 = [tspec, tspec], [jax.ShapeDtypeStruct((grp, d, rows), F32)] * 2
    else:
        blk = min(rows, ATTN_BLOCK)
        specs2 = [xspec, pl.BlockSpec((1, tile_m // blk, d, blk), lambda b, t: (b, t, 0, 0))]
        sds2 = [sds(BF16), jax.ShapeDtypeStruct((grp, rows // blk, d, blk), BF16)]
    return pl.pallas_call(
        functools.partial(_kvq_kernel, decode, mod),
        grid=(grp, rows // tile_m),
        in_specs=[xspec, mod.spec(0), mod.spec(1),
                  _resident((d, 2 * d), lambda b, t: (0, 0)), _resident((1, d, d), lambda b, t: (0, 0, 0))],
        out_specs=[xspec, xspec] + specs2 + [xspec],
        out_shape=[sds(F32), sds(F32)] + sds2 + [sds(F32 if decode else BF16)],
        compiler_params=_params("parallel", "parallel"),
        name="kvq",
    )(x, ada, ada, wkv_bf, wq_bf)


SOFTPLUS_LINEAR_FROM = 30.0


def _softplus(z):
    return jnp.maximum(jnp.log(1.0 + jnp.exp(jnp.minimum(z, SOFTPLUS_LINEAR_FROM))), z)


def _triangle(n, transposed=False):
    r = lax.broadcasted_iota(jnp.int32, (n, n), 0)
    c = lax.broadcasted_iota(jnp.int32, (n, n), 1)
    return jnp.where((c > r) if transposed else (r > c), 1.0, 0.0).astype(BF16)


def _attn_kernel(blk, bias_ref, q_ref, k_ref, vt_ref, o_ref):
    group = pl.program_id(1)
    i = pl.program_id(2)
    lane = lax.broadcasted_iota(jnp.int32, (1, SLAB), 1)
    key = lax.broadcasted_iota(jnp.int32, (blk, blk), 0)
    qry = lax.broadcasted_iota(jnp.int32, (blk, blk), 1)
    causal = key < qry
    tri = _triangle(blk, transposed=True)
    heads = [(sl, hh) for sl in range(ATTN_SLABS_PER_STEP) for hh in range(HEADS_PER_SLAB)]
    qms, biases = [], []
    for sl, hh in heads:
        q = q_ref[0, :, sl * SLAB:(sl + 1) * SLAB]
        in_head = (lane >= hh * HEAD_DIM) & (lane < (hh + 1) * HEAD_DIM)
        qms.append(jnp.where(in_head, q, jnp.zeros_like(q)))
        biases.append(bias_ref[(group * ATTN_SLABS_PER_STEP + sl) * HEADS_PER_SLAB + hh])

    def tile(j, carry, mask):
        start = pl.multiple_of(j * blk, blk)
        new = []
        for w0 in range(0, len(heads), ATTN_HEADS_PER_WAVE):
            wave = list(range(w0, min(w0 + ATTN_HEADS_PER_WAVE, len(heads))))
            zs = [lax.dot_general(k_ref[0, pl.ds(start, blk), heads[c][0] * SLAB:(heads[c][0] + 1) * SLAB],
                                  qms[c], (((1,), (1,)), ((), ())), preferred_element_type=F32) + biases[c]
                  for c in wave]
            nks = [_softplus(z) for z in zs]
            if mask is not None:
                nks = [jnp.where(mask, nk, 0.0) for nk in nks]
            log_betas = [z - nk for z, nk in zip(zs, nks)]
            first = [nk[0:1, :] for nk in nks]
            excls = [jnp.dot(tri, nk.astype(BF16), preferred_element_type=F32) for nk in nks]
            ws = [jnp.exp(lb - excl - carry[c][0]) for c, lb, excl in zip(wave, log_betas, excls)]
            if mask is not None:
                ws = [jnp.where(mask, a, 0.0) for a in ws]
            for n, c in enumerate(wave):
                lo = heads[c][0] * SLAB + heads[c][1] * HEAD_DIM
                vt = vt_ref[0, j, lo:lo + HEAD_DIM, :]
                new.append((carry[c][0] + (excls[n][0:1, :] + first[n]),
                            carry[c][1] + jnp.dot(vt, ws[n].astype(BF16), preferred_element_type=F32)))
        return tuple(new)

    init = tuple((jnp.zeros((1, blk), F32), jnp.zeros((HEAD_DIM, blk), F32)) for _ in heads)
    carry = tile(i, init, causal)
    carry = lax.fori_loop(0, i, lambda n, c: tile(i - 1 - n, c, None), carry)
    for sl in range(ATTN_SLABS_PER_STEP):
        ot = jnp.concatenate([carry[HEADS_PER_SLAB * sl + hh][1] for hh in range(HEADS_PER_SLAB)], axis=0)
        o_ref[0, :, sl * SLAB:(sl + 1) * SLAB] = ot.T.astype(o_ref.dtype)


def _attn_prompt(q_bf, k_bf, vt_bf, bias):
    bsz, seq, d = q_bf.shape
    blk = vt_bf.shape[3]
    width = ATTN_SLABS_PER_STEP * SLAB
    qspec = pl.BlockSpec((1, blk, width), lambda b, p, i: (b, i, p))
    kspec = pl.BlockSpec((1, seq, width), lambda b, p, i: (b, 0, p))
    vspec = pl.BlockSpec((1, seq // blk, width, blk), lambda b, p, i: (b, 0, p, 0))
    return pl.pallas_call(
        functools.partial(_attn_kernel, blk),
        grid=(bsz, N_SLABS // ATTN_SLABS_PER_STEP, seq // blk),
        in_specs=[pl.BlockSpec(memory_space=pltpu.SMEM), qspec, kspec, vspec],
        out_specs=qspec,
        out_shape=jax.ShapeDtypeStruct((bsz, seq, d), BF16),
        compiler_params=_params("parallel", "parallel", "arbitrary"),
        name="attn_prompt",
    )(bias, q_bf, k_bf, vt_bf)


def _decode_kernel(n_pages_step, *refs):
    pt_ref = refs[0]
    q_ref, bias_ref, qpos_ref, kn_ref, vn_ref = refs[1:6]
    k_refs = refs[6:6 + n_pages_step]
    v_refs = refs[6 + n_pages_step:6 + 2 * n_pages_step]
    o_ref, r_ref, acc_ref = refs[6 + 2 * n_pages_step:]
    del pt_ref
    step = pl.program_id(1)
    t_new = q_ref.shape[1]
    rows = N_SLABS * HEADS_PER_SLAB * t_new
    slab_rows = HEADS_PER_SLAB * t_new

    q = q_ref[0]
    lane = lax.broadcasted_iota(jnp.int32, (1, SLAB), 1)
    qms = []
    for p in range(N_SLABS):
        qp = q[:, p * SLAB:(p + 1) * SLAB]
        parts = [jnp.where((lane >= hh * HEAD_DIM) & (lane < (hh + 1) * HEAD_DIM), qp, 0.0)
                 for hh in range(HEADS_PER_SLAB)]
        qms.append(jnp.concatenate(parts, axis=0).astype(BF16))
    bias = bias_ref[...]
    tri = _triangle(PAGE_SIZE)

    def pages(kts, vts, mask):
        zs = [jnp.concatenate(
            [jnp.dot(qms[p], kt[0, p * SLAB:(p + 1) * SLAB, :].astype(BF16), preferred_element_type=F32)
             for p in range(N_SLABS)], axis=0) + bias for kt in kts]
        nks = [_softplus(z) for z in zs]
        if mask is not None:
            nks = [jnp.where(mask, nk, 0.0) for nk in nks]
        excls = [jnp.dot(nk.astype(BF16), tri, preferred_element_type=F32) for nk in nks]
        r = r_ref[...]
        ws = []
        for z, nk, excl in zip(zs, nks, excls):
            a = jnp.exp((z - nk) - excl - r)
            if mask is not None:
                a = jnp.where(mask, a, 0.0)
            ws.append(a.astype(BF16))
            r = r + (excl[:, 0:1] + nk[:, 0:1])
        r_ref[...] = r
        for p in range(N_SLABS):
            part = acc_ref[p * slab_rows:(p + 1) * slab_rows, :]
            for w, vt in zip(ws, vts):
                part = part + jnp.dot(
                    w[p * slab_rows:(p + 1) * slab_rows, :], vt[0, p * SLAB:(p + 1) * SLAB, :].T.astype(BF16),
                    preferred_element_type=F32)
            acc_ref[p * slab_rows:(p + 1) * slab_rows, :] = part

    @pl.when(step == 0)
    def _():
        r_ref[...] = jnp.zeros_like(r_ref)
        acc_ref[...] = jnp.zeros_like(acc_ref)
        off = (pl.program_id(0) * t_new) % PAGE_SIZE
        key = lax.broadcasted_iota(jnp.int32, (rows, PAGE_SIZE), 1) - off
        pages([kn_ref], [vn_ref], (key >= 0) & (key < qpos_ref[...]))

    order = list(reversed(range(n_pages_step)))
    pages([k_refs[g] for g in order], [v_refs[g] for g in order], None)

    @pl.when(step == pl.num_programs(1) - 1)
    def _():
        for p in range(N_SLABS):
            blk = acc_ref[p * slab_rows:(p + 1) * slab_rows, :]
            o_ref[0, :, p * SLAB:(p + 1) * SLAB] = jnp.where(
                lane < HEAD_DIM, blk[:t_new], blk[t_new:]).astype(o_ref.dtype)


def _attn_decode(q, kt_new, vt_new, cache_k, cache_v, page_table, bias_rows, qpos_rows):
    bsz, t_new, d = q.shape
    assert PAGE_SIZE % t_new == 0 and kt_new.shape[2] % PAGE_SIZE == 0
    n_pages = page_table.shape[1]
    g = DECODE_PAGES_PER_STEP
    while n_pages % g:
        g //= 2
    n_steps = n_pages // g
    rows = N_HEADS * t_new

    def page_spec(slot):
        return pl.BlockSpec((1, d, PAGE_SIZE), lambda b, s, pt: (pt[b, (n_steps - 1 - s) * g + slot], 0, 0))

    per_b = lambda shape: pl.BlockSpec(shape, lambda b, s, pt: (b, 0, 0))
    new_spec = pl.BlockSpec((1, d, PAGE_SIZE), lambda b, s, pt: (0, 0, (b * t_new) // PAGE_SIZE))
    grid_spec = pltpu.PrefetchScalarGridSpec(
        num_scalar_prefetch=1,
        grid=(bsz, n_steps),
        in_specs=[per_b((1, t_new, d)), pl.BlockSpec((rows, 1), lambda b, s, pt: (0, 0)),
                  pl.BlockSpec((rows, 1), lambda b, s, pt: (0, 0)), new_spec, new_spec]
                 + [page_spec(slot) for slot in range(g)] * 2,
        out_specs=per_b((1, t_new, d)),
        scratch_shapes=[pltpu.VMEM((rows, 1), F32), pltpu.VMEM((rows, SLAB), F32)],
    )
    return pl.pallas_call(
        functools.partial(_decode_kernel, g),
        grid_spec=grid_spec,
        out_shape=jax.ShapeDtypeStruct((bsz, t_new, d), BF16),
        compiler_params=_params("parallel", "arbitrary"),
        name="attn_decode",
    )(page_table, q, bias_rows, qpos_rows, kt_new, vt_new, *([cache_k] * g), *([cache_v] * g))


def kernel(x_prompt, x_sample, cache_k, cache_v, state_pool, page_table, c_prompt, c_sample,
           w_ada, b_ada, ln_g, ln_b, w_pool, pool_scale, w_up, w_down, w_kv, w_q, w_o, b_break):
    bsz, seq, d = x_prompt.shape
    dec_b, dec_t, _ = x_sample.shape
    assert d == D_MODEL and w_ada.shape[0] == DEPTH == 2 and w_pool.shape[0] == 1 and w_q.shape[0] == 1
    past_len = page_table.shape[1] * PAGE_SIZE

    wu_bf, wd_bf = w_up.astype(BF16), w_down.astype(BF16)
    wkv_bf, wq_bf, wo_bf, wp_bf = w_kv.astype(BF16), w_q.astype(BF16), w_o.astype(BF16), w_pool.astype(BF16)
    ln_g4 = ln_g.reshape(DEPTH * 2, 1, d)
    ln_b4 = ln_b.reshape(DEPTH * 2, 1, d)

    n_tok = dec_b * dec_t
    assert n_tok % MOD_ROW_GROUP == 0
    pad_rows = lambda c: jnp.pad(c, ((0, -c.shape[0] % MOD_ROW_GROUP), (0, 0)))
    sections = [jnp.repeat(c_sample, dec_t, axis=0), pad_rows(c_prompt), pad_rows(c_sample)]
    prompt_row, seq_row = n_tok, n_tok + sections[1].shape[0]
    ada = _ada(jnp.concatenate(sections, axis=0), w_ada, b_ada)
    n_seq = MIXER_SEQS_PER_STEP
    while dec_b % n_seq:
        n_seq //= 2
    mod_p = [_ModRows(l, MOD_ROW_GROUP, lambda b: prompt_row + b, 1) for l in range(DEPTH)]
    mod_s = [_ModRows(l, n_tok, lambda b: 0, None) for l in range(DEPTH)]
    mod_seq = _ModRows(0, MOD_ROW_GROUP, lambda b: seq_row + b * n_seq, n_seq)
    xs_flat = lambda a: a.reshape(1, n_tok, d)

    zero_hist = jnp.zeros((bsz, POOL_HIST, d), F32)
    x1_p, hist_p = _mixer(x_prompt, zero_hist, ada, mod_p[0], 0, wp_bf, pool_scale, ln_g4, ln_b4, 0)
    x1_s, hist_s = _mixer(x_sample, state_pool[0], ada, mod_seq, past_len, wp_bf, pool_scale, ln_g4, ln_b4, 0)
    x2_p = _mlp(x1_p, ada, mod_p[0], wu_bf, wd_bf, ln_g4, ln_b4, 1)
    x2_s = _mlp(xs_flat(x1_s), ada, mod_s[0], wu_bf, wd_bf, ln_g4, ln_b4, 1)

    k_p, v_p, kb_p, vtb_p, q_p = _kvq(x2_p, ada, mod_p[1], wkv_bf, wq_bf, decode=False)
    k_s, v_s, kt_s, vt_s, q_s = _kvq(x2_s, ada, mod_s[1], wkv_bf, wq_bf, decode=True)

    bias = b_break[0].astype(F32)
    o_p = _attn_prompt(q_p, kb_p, vtb_p, bias)
    pages = lambda c: jnp.transpose(c, (0, 2, 3, 1)).reshape(-1, d, PAGE_SIZE)
    pad_cols = -n_tok % PAGE_SIZE
    if pad_cols:
        kt_s, vt_s = (jnp.pad(a, ((0, 0), (0, 0), (0, pad_cols))) for a in (kt_s, vt_s))
    o_s = _attn_decode(q_s.reshape(dec_b, dec_t, d), kt_s, vt_s, pages(cache_k), pages(cache_v),
                       page_table, jnp.repeat(bias, dec_t)[:, None],
                       jnp.tile(jnp.arange(dec_t, dtype=jnp.int32), N_HEADS)[:, None])

    y_p = _mlp(x2_p, ada, mod_p[1], wu_bf, wd_bf, ln_g4, ln_b4, 3, oproj=(o_p, wo_bf, 2))
    y_s = _mlp(x2_s, ada, mod_s[1], wu_bf, wd_bf, ln_g4, ln_b4, 3, oproj=(xs_flat(o_s), wo_bf, 2))

    heads = lambda a, b, t: a.reshape(b, t, N_HEADS, HEAD_DIM)
    return (y_p, y_s.reshape(dec_b, dec_t, d),
            heads(k_p, bsz, seq), heads(v_p, bsz, seq), hist_p[None],
            heads(k_s, dec_b, dec_t), heads(v_s, dec_b, dec_t), hist_s[None])
```

```python
import functools

import jax
import jax.numpy as jnp
from jax import lax
from jax.experimental import pallas as pl
from jax.experimental.pallas import tpu as pltpu

F32 = jnp.float32
BF16 = jnp.bfloat16

D_MODEL = 1024
DEPTH = 2
N_HEADS = 16
HEAD_DIM = 64
HEADS_PER_SLAB = 2
SLAB = HEADS_PER_SLAB * HEAD_DIM
N_SLABS = N_HEADS // HEADS_PER_SLAB
D_FF = 4 * D_MODEL
POOL_WINDOWS = (2, 4, 8, 16)
POOL_GROUP_DIM = D_MODEL // len(POOL_WINDOWS)
POOL_HIST = max(POOL_WINDOWS) - 1
HIST_ROWS = 2 * max(POOL_WINDOWS)
MOD_ROW_GROUP = 8
MIXER_TILE_T = 1024
MIXER_SEQS_PER_STEP = 8
LN_EPS = 1e-5
DN_ALPHA = float((2 * DEPTH) ** 0.25)
ATTN_SCALE = HEAD_DIM ** -0.5
PAGE_SIZE = 128

VMEM_LIMIT_BYTES = 56 * 1024 * 1024
ADA_TILE_N = 1536
MLP_TILE_M = 512
MLP_TILE_F = 512
ATTN_BLOCK = 256
ATTN_SLABS_PER_STEP = 8
ATTN_HEADS_PER_WAVE = 16
DECODE_PAGES_PER_STEP = 16


def _params(*sem):
    return pltpu.CompilerParams(dimension_semantics=sem, vmem_limit_bytes=VMEM_LIMIT_BYTES)


def _layer_norm(y, g, b):
    mu = jnp.mean(y, axis=-1, keepdims=True)
    yc = y - mu
    var = jnp.mean(yc * yc, axis=-1, keepdims=True)
    return yc * lax.rsqrt(var + LN_EPS) * g + b


def _resident(shape, index_map):
    return pl.BlockSpec(shape, index_map, pipeline_mode=pl.Buffered(1))


def _ada_kernel(c_ref, w_ref, b_ref, o_ref):
    c = c_ref[...]
    s = c * jax.nn.sigmoid(c)
    o_ref[0] = jnp.dot(s.astype(BF16), w_ref[0].astype(BF16), preferred_element_type=F32) + b_ref[0]


def _ada(c_all, w_ada, b_ada):
    n_layers, d, n = w_ada.shape
    m = c_all.shape[0]
    return pl.pallas_call(
        _ada_kernel,
        grid=(n_layers, n // ADA_TILE_N),
        in_specs=[
            pl.BlockSpec((m, d), lambda l, j: (0, 0)),
            pl.BlockSpec((1, d, ADA_TILE_N), lambda l, j: (l, 0, j)),
            pl.BlockSpec((1, 1, ADA_TILE_N), lambda l, j: (l, 0, j)),
        ],
        out_specs=pl.BlockSpec((1, m, ADA_TILE_N), lambda l, j: (l, 0, j)),
        out_shape=jax.ShapeDtypeStruct((n_layers, m, n), F32),
        compiler_params=_params("parallel", "parallel"),
        name="ada",
    )(c_all, w_ada, b_ada.reshape(n_layers, 1, n))


class _ModRows:
    def __init__(self, layer, rows, first_row, per_step):
        self.layer, self.rows, self.first_row, self.per_step = layer, rows, first_row, per_step

    def spec(self, chunk):
        return pl.BlockSpec((1, self.rows, D_MODEL),
                            lambda b, t: (self.layer, self.first_row(b) // self.rows, chunk))

    def read(self, ref, s=0):
        if self.per_step is None:
            return ref[0]
        return ref[0, pl.ds((pl.program_id(0) * self.per_step) % self.rows + s, 1), :]


def _mixer_kernel(pos0, tile_t, n_seq, mod, x_ref, hist_ref, sh_ref, sc_ref, g_ref, wp_ref, ps_ref,
                  lng_ref, lnb_ref, o_ref, ho_ref, ext_ref, s2_ref, s4_ref, s8_ref):
    t = pl.program_id(1)
    gd = POOL_GROUP_DIM
    rows_seq = HIST_ROWS + tile_t
    n = n_seq * rows_seq

    @pl.when(t == 0)
    def _():
        for s in range(n_seq):
            ext_ref[s * rows_seq:s * rows_seq + HIST_ROWS - POOL_HIST, :] = jnp.zeros(
                (HIST_ROWS - POOL_HIST, D_MODEL), F32)
            ext_ref[s * rows_seq + HIST_ROWS - POOL_HIST:s * rows_seq + HIST_ROWS, :] = hist_ref[s]

    for s in range(n_seq):
        ext_ref[s * rows_seq + HIST_ROWS:(s + 1) * rows_seq, :] = (
            x_ref[s] * (1.0 + mod.read(sc_ref, s)) + mod.read(sh_ref, s))
    s2_ref[8:n, :] = ext_ref[8:n, gd:] + ext_ref[7:n - 1, gd:]
    s4_ref[16:n, :] = s2_ref[16:n, :] + s2_ref[14:n - 2, :]
    s8_ref[24:n, :] = s4_ref[24:n, gd:] + s4_ref[20:n - 4, gd:]

    pos = pos0 + t * tile_t + lax.broadcasted_iota(jnp.int32, (tile_t, 1), 0)
    inv_cnt = [1.0 / jnp.minimum(w, pos + 1).astype(F32) for w in POOL_WINDOWS]
    ds = [[] for _ in POOL_WINDOWS]
    for s in range(n_seq):
        a, b = s * rows_seq + HIST_ROWS, (s + 1) * rows_seq
        wsums = [ext_ref[a:b, :gd] + ext_ref[a - 1:b - 1, :gd], s4_ref[a:b, :gd], s8_ref[a:b, :gd],
                 s8_ref[a:b, gd:] + s8_ref[a - 8:b - 8, gd:]]
        for g, wsum in enumerate(wsums):
            ds[g].append(wsum * inv_cnt[g] - ext_ref[a:b, g * gd:(g + 1) * gd])
    mix = jnp.concatenate(
        [jnp.dot(jnp.concatenate(ds[g], axis=0).astype(BF16), wp_ref[0, g], preferred_element_type=F32)
         for g in range(len(POOL_WINDOWS))], axis=-1) * ps_ref[...]
    for s in range(n_seq):
        y = DN_ALPHA * x_ref[s] + mod.read(g_ref, s) * mix[s * tile_t:(s + 1) * tile_t]
        o_ref[s] = _layer_norm(y, lng_ref[0], lnb_ref[0])
        ho_ref[s] = ext_ref[(s + 1) * rows_seq - POOL_HIST:(s + 1) * rows_seq, :]
        ext_ref[s * rows_seq:s * rows_seq + HIST_ROWS, :] = ext_ref[(s + 1) * rows_seq - HIST_ROWS:
                                                                    (s + 1) * rows_seq, :]


def _mixer(x, hist, ada, mod, pos0, wp_bf, pool_scale, ln_g, ln_b, ln_idx):
    bsz, seq, d = x.shape
    tile_t = min(seq, MIXER_TILE_T)
    n_seq = mod.per_step
    assert bsz % n_seq == 0 and tile_t % 8 == 0
    n = n_seq * (HIST_ROWS + tile_t)
    gd = POOL_GROUP_DIM
    return pl.pallas_call(
        functools.partial(_mixer_kernel, pos0, tile_t, n_seq, mod),
        grid=(bsz // n_seq, seq // tile_t),
        in_specs=[
            pl.BlockSpec((n_seq, tile_t, d), lambda b, t: (b, t, 0)),
            pl.BlockSpec((n_seq, POOL_HIST, d), lambda b, t: (b, 0, 0)),
            mod.spec(0), mod.spec(1), mod.spec(2),
            pl.BlockSpec(wp_bf.shape, lambda b, t: (0, 0, 0, 0)),
            pl.BlockSpec((1, d), lambda b, t: (0, 0)),
            pl.BlockSpec((1, 1, d), lambda b, t: (ln_idx, 0, 0)),
            pl.BlockSpec((1, 1, d), lambda b, t: (ln_idx, 0, 0)),
        ],
        out_specs=[
            pl.BlockSpec((n_seq, tile_t, d), lambda b, t: (b, t, 0)),
            pl.BlockSpec((n_seq, POOL_HIST, d), lambda b, t: (b, 0, 0)),
        ],
        out_shape=[jax.ShapeDtypeStruct((bsz, seq, d), F32),
                   jax.ShapeDtypeStruct((bsz, POOL_HIST, d), F32)],
        scratch_shapes=[pltpu.VMEM((n, d), F32), pltpu.VMEM((n, d - gd), F32),
                        pltpu.VMEM((n, d - gd), F32), pltpu.VMEM((n, d - 2 * gd), F32)],
        compiler_params=_params("parallel", "arbitrary"),
        name="mixer",
    )(x, hist, ada, ada, ada, wp_bf, pool_scale, ln_g, ln_b)


def _mlp_kernel(has_oproj, mod, *refs):
    if has_oproj:
        (x_ref, o_ref, g1_ref, wo_ref, lng1_ref, lnb1_ref,
         sh_ref, sc_ref, g_ref, wu_ref, wd_ref, lng_ref, lnb_ref, out_ref, acc_ref) = refs
        mix = jnp.dot(o_ref[0], wo_ref[0], preferred_element_type=F32)
        x = _layer_norm(DN_ALPHA * x_ref[0] + mod.read(g1_ref) * mix, lng1_ref[0], lnb1_ref[0])
    else:
        (x_ref, sh_ref, sc_ref, g_ref, wu_ref, wd_ref, lng_ref, lnb_ref, out_ref, acc_ref) = refs
        x = x_ref[0]
    h = (x * (1.0 + mod.read(sc_ref)) + mod.read(sh_ref)).astype(BF16)
    for c in range(D_FF // MLP_TILE_F):
        lo, hi = c * MLP_TILE_F, (c + 1) * MLP_TILE_F
        u = jnp.dot(h, wu_ref[0, :, lo:hi], preferred_element_type=F32)
        u = jnp.maximum(u, 0.0)
        contrib = jnp.dot((u * u).astype(BF16), wd_ref[0, lo:hi, :], preferred_element_type=F32)
        if c == 0:
            acc_ref[...] = contrib
        else:
            acc_ref[...] += contrib
    y = DN_ALPHA * x + mod.read(g_ref) * acc_ref[...]
    out_ref[0] = _layer_norm(y, lng_ref[0], lnb_ref[0])


def _mlp(x, ada, mod, wu_bf, wd_bf, ln_g, ln_b, ln_idx, oproj=None):
    grp, rows, d = x.shape
    tile_m = min(rows, MLP_TILE_M)
    xspec = pl.BlockSpec((1, tile_m, d), lambda b, t: (b, t, 0))
    lnspec = lambda idx: pl.BlockSpec((1, 1, d), lambda b, t: (idx, 0, 0))
    ins, specs = [x], [xspec]
    if oproj is not None:
        o_bf, wo_bf, ln_idx1 = oproj
        ins += [o_bf, ada, wo_bf, ln_g, ln_b]
        specs += [xspec, mod.spec(2), _resident((1, d, d), lambda b, t: (0, 0, 0)),
                  lnspec(ln_idx1), lnspec(ln_idx1)]
    ins += [ada, ada, ada, wu_bf, wd_bf, ln_g, ln_b]
    specs += [mod.spec(3), mod.spec(4), mod.spec(5),
              _resident((1, d, D_FF), lambda b, t: (mod.layer, 0, 0)),
              _resident((1, D_FF, d), lambda b, t: (mod.layer, 0, 0)),
              lnspec(ln_idx), lnspec(ln_idx)]
    return pl.pallas_call(
        functools.partial(_mlp_kernel, oproj is not None, mod),
        grid=(grp, rows // tile_m),
        in_specs=specs,
        out_specs=xspec,
        out_shape=jax.ShapeDtypeStruct((grp, rows, d), F32),
        scratch_shapes=[pltpu.VMEM((tile_m, d), F32)],
        compiler_params=_params("parallel", "parallel"),
        name="mlp_oproj" if oproj is not None else "mlp",
    )(*ins)


def _kvq_kernel(decode, mod, x_ref, sh_ref, sc_ref, wkv_ref, wq_ref, k_ref, v_ref, k2_ref, v2_ref, q_ref):
    x = x_ref[0]
    kv = jnp.dot(x.astype(BF16), wkv_ref[...], preferred_element_type=F32)
    k, v = kv[:, :D_MODEL], kv[:, D_MODEL:]
    k_ref[0] = k
    v_ref[0] = v
    if decode:
        k2_ref[0] = k.T
        v2_ref[0] = v.T
    else:
        k2_ref[0] = k.astype(BF16)
        blk = v2_ref.shape[3]
        for kb in range(v2_ref.shape[1]):
            v2_ref[0, kb] = v[kb * blk:(kb + 1) * blk, :].T.astype(BF16)
    h = (x * (1.0 + mod.read(sc_ref)) + mod.read(sh_ref)).astype(BF16)
    q_ref[0] = (jnp.dot(h, wq_ref[0], preferred_element_type=F32) * ATTN_SCALE).astype(q_ref.dtype)


def _kvq(x, ada, mod, wkv_bf, wq_bf, decode):
    grp, rows, d = x.shape
    tile_m = min(rows, MLP_TILE_M)
    xspec = pl.BlockSpec((1, tile_m, d), lambda b, t: (b, t, 0))
    sds = lambda dt: jax.ShapeDtypeStruct((grp, rows, d), dt)
    if decode:
        tspec = pl.BlockSpec((1, d, tile_m), lambda b, t: (b, 0, t))
        specs2, sds2 = [tspec, tspec], [jax.ShapeDtypeStruct((grp, d, rows), F32)] * 2
    else:
        blk = min(rows, ATTN_BLOCK)
        specs2 = [xspec, pl.BlockSpec((1, tile_m // blk, d, blk), lambda b, t: (b, t, 0, 0))]
        sds2 = [sds(BF16), jax.ShapeDtypeStruct((grp, rows // blk, d, blk), BF16)]
    return pl.pallas_call(
        functools.partial(_kvq_kernel, decode, mod),
        grid=(grp, rows // tile_m),
        in_specs=[xspec, mod.spec(0), mod.spec(1),
                  _resident((d, 2 * d), lambda b, t: (0, 0)), _resident((1, d, d), lambda b, t: (0, 0, 0))],
        out_specs=[xspec, xspec] + specs2 + [xspec],
        out_shape=[sds(F32), sds(F32)] + sds2 + [sds(F32 if decode else BF16)],
        compiler_params=_params("parallel", "parallel"),
        name="kvq",
    )(x, ada, ada, wkv_bf, wq_bf)


SOFTPLUS_LINEAR_FROM = 30.0


def _softplus(z):
    return jnp.maximum(jnp.log(1.0 + jnp.exp(jnp.minimum(z, SOFTPLUS_LINEAR_FROM))), z)


def _triangle(n, transposed=False):
    r = lax.broadcasted_iota(jnp.int32, (n, n), 0)
    c = lax.broadcasted_iota(jnp.int32, (n, n), 1)
    return jnp.where((c > r) if transposed else (r > c), 1.0, 0.0).astype(BF16)


def _attn_kernel(blk, bias_ref, q_ref, k_ref, vt_ref, o_ref):
    group = pl.program_id(1)
    i = pl.program_id(2)
    lane = lax.broadcasted_iota(jnp.int32, (1, SLAB), 1)
    key = lax.broadcasted_iota(jnp.int32, (blk, blk), 0)
    qry = lax.broadcasted_iota(jnp.int32, (blk, blk), 1)
    causal = key < qry
    tri = _triangle(blk, transposed=True)
    heads = [(sl, hh) for sl in range(ATTN_SLABS_PER_STEP) for hh in range(HEADS_PER_SLAB)]
    qms, biases = [], []
    for sl, hh in heads:
        q = q_ref[0, :, sl * SLAB:(sl + 1) * SLAB]
        in_head = (lane >= hh * HEAD_DIM) & (lane < (hh + 1) * HEAD_DIM)
        qms.append(jnp.where(in_head, q, jnp.zeros_like(q)))
        biases.append(bias_ref[(group * ATTN_SLABS_PER_STEP + sl) * HEADS_PER_SLAB + hh])

    def tile(j, carry, mask):
        start = pl.multiple_of(j * blk, blk)
        new = []
        for w0 in range(0, len(heads), ATTN_HEADS_PER_WAVE):
            wave = list(range(w0, min(w0 + ATTN_HEADS_PER_WAVE, len(heads))))
            zs = [lax.dot_general(k_ref[0, pl.ds(start, blk), heads[c][0] * SLAB:(heads[c][0] + 1) * SLAB],
                                  qms[c], (((1,), (1,)), ((), ())), preferred_element_type=F32) + biases[c]
                  for c in wave]
            nks = [_softplus(z) for z in zs]
            if mask is not None:
                nks = [jnp.where(mask, nk, 0.0) for nk in nks]
            log_betas = [z - nk for z, nk in zip(zs, nks)]
            first = [nk[0:1, :] for nk in nks]
            excls = [jnp.dot(tri, nk.astype(BF16), preferred_element_type=F32) for nk in nks]
            ws = [jnp.exp(lb - excl - carry[c][0]) for c, lb, excl in zip(wave, log_betas, excls)]
            if mask is not None:
                ws = [jnp.where(mask, a, 0.0) for a in ws]
            for n, c in enumerate(wave):
                lo = heads[c][0] * SLAB + heads[c][1] * HEAD_DIM
                vt = vt_ref[0, j, lo:lo + HEAD_DIM, :]
                new.append((carry[c][0] + (excls[n][0:1, :] + first[n]),
                            carry[c][1] + jnp.dot(vt, ws[n].astype(BF16), preferred_element_type=F32)))
        return tuple(new)

    init = tuple((jnp.zeros((1, blk), F32), jnp.zeros((HEAD_DIM, blk), F32)) for _ in heads)
    carry = tile(i, init, causal)
    carry = lax.fori_loop(0, i, lambda n, c: tile(i - 1 - n, c, None), carry)
    for sl in range(ATTN_SLABS_PER_STEP):
        ot = jnp.concatenate([carry[HEADS_PER_SLAB * sl + hh][1] for hh in range(HEADS_PER_SLAB)], axis=0)
        o_ref[0, :, sl * SLAB:(sl + 1) * SLAB] = ot.T.astype(o_ref.dtype)


def _attn_prompt(q_bf, k_bf, vt_bf, bias):
    bsz, seq, d = q_bf.shape
    blk = vt_bf.shape[3]
    width = ATTN_SLABS_PER_STEP * SLAB
    qspec = pl.BlockSpec((1, blk, width), lambda b, p, i: (b, i, p))
    kspec = pl.BlockSpec((1, seq, width), lambda b, p, i: (b, 0, p))
    vspec = pl.BlockSpec((1, seq // blk, width, blk), lambda b, p, i: (b, 0, p, 0))
    return pl.pallas_call(
        functools.partial(_attn_kernel, blk),
        grid=(bsz, N_SLABS // ATTN_SLABS_PER_STEP, seq // blk),
        in_specs=[pl.BlockSpec(memory_space=pltpu.SMEM), qspec, kspec, vspec],
        out_specs=qspec,
        out_shape=jax.ShapeDtypeStruct((bsz, seq, d), BF16),
        compiler_params=_params("parallel", "parallel", "arbitrary"),
        name="attn_prompt",
    )(bias, q_bf, k_bf, vt_bf)


def _decode_kernel(n_pages_step, *refs):
    pt_ref = refs[0]
    q_ref, bias_ref, qpos_ref, kn_ref, vn_ref = refs[1:6]
    k_refs = refs[6:6 + n_pages_step]
    v_refs = refs[6 + n_pages_step:6 + 2 * n_pages_step]
    o_ref, r_ref, acc_ref = refs[6 + 2 * n_pages_step:]
    del pt_ref
    step = pl.program_id(1)
    t_new = q_ref.shape[1]
    rows = N_SLABS * HEADS_PER_SLAB * t_new
    slab_rows = HEADS_PER_SLAB * t_new

    q = q_ref[0]
    lane = lax.broadcasted_iota(jnp.int32, (1, SLAB), 1)
    qms = []
    for p in range(N_SLABS):
        qp = q[:, p * SLAB:(p + 1) * SLAB]
        parts = [jnp.where((lane >= hh * HEAD_DIM) & (lane < (hh + 1) * HEAD_DIM), qp, 0.0)
                 for hh in range(HEADS_PER_SLAB)]
        qms.append(jnp.concatenate(parts, axis=0).astype(BF16))
    bias = bias_ref[...]
    tri = _triangle(PAGE_SIZE)

    def pages(kts, vts, mask):
        zs = [jnp.concatenate(
            [jnp.dot(qms[p], kt[0, p * SLAB:(p + 1) * SLAB, :].astype(BF16), preferred_element_type=F32)
             for p in range(N_SLABS)], axis=0) + bias for kt in kts]
        nks = [_softplus(z) for z in zs]
        if mask is not None:
            nks = [jnp.where(mask, nk, 0.0) for nk in nks]
        excls = [jnp.dot(nk.astype(BF16), tri, preferred_element_type=F32) for nk in nks]
        r = r_ref[...]
        ws = []
        for z, nk, excl in zip(zs, nks, excls):
            a = jnp.exp((z - nk) - excl - r)
            if mask is not None:
                a = jnp.where(mask, a, 0.0)
            ws.append(a.astype(BF16))
            r = r + (excl[:, 0:1] + nk[:, 0:1])
        r_ref[...] = r
        for p in range(N_SLABS):
            part = acc_ref[p * slab_rows:(p + 1) * slab_rows, :]
            for w, vt in zip(ws, vts):
                part = part + jnp.dot(
                    w[p * slab_rows:(p + 1) * slab_rows, :], vt[0, p * SLAB:(p + 1) * SLAB, :].T.astype(BF16),
                    preferred_element_type=F32)
            acc_ref[p * slab_rows:(p + 1) * slab_rows, :] = part

    @pl.when(step == 0)
    def _():
        r_ref[...] = jnp.zeros_like(r_ref)
        acc_ref[...] = jnp.zeros_like(acc_ref)
        off = (pl.program_id(0) * t_new) % PAGE_SIZE
        key = lax.broadcasted_iota(jnp.int32, (rows, PAGE_SIZE), 1) - off
        pages([kn_ref], [vn_ref], (key >= 0) & (key < qpos_ref[...]))

    order = list(reversed(range(n_pages_step)))
    pages([k_refs[g] for g in order], [v_refs[g] for g in order], None)

    @pl.when(step == pl.num_programs(1) - 1)
    def _():
        for p in range(N_SLABS):
            blk = acc_ref[p * slab_rows:(p + 1) * slab_rows, :]
            o_ref[0, :, p * SLAB:(p + 1) * SLAB] = jnp.where(
                lane < HEAD_DIM, blk[:t_new], blk[t_new:]).astype(o_ref.dtype)


def _attn_decode(q, kt_new, vt_new, cache_k, cache_v, page_table, bias_rows, qpos_rows):
    bsz, t_new, d = q.shape
    assert PAGE_SIZE % t_new == 0 and kt_new.shape[2] % PAGE_SIZE == 0
    n_pages = page_table.shape[1]
    g = DECODE_PAGES_PER_STEP
    while n_pages % g:
        g //= 2
    n_steps = n_pages // g
    rows = N_HEADS * t_new

    def page_spec(slot):
        return pl.BlockSpec((1, d, PAGE_SIZE), lambda b, s, pt: (pt[b, (n_steps - 1 - s) * g + slot], 0, 0))

    per_b = lambda shape: pl.BlockSpec(shape, lambda b, s, pt: (b, 0, 0))
    new_spec = pl.BlockSpec((1, d, PAGE_SIZE), lambda b, s, pt: (0, 0, (b * t_new) // PAGE_SIZE))
    grid_spec = pltpu.PrefetchScalarGridSpec(
        num_scalar_prefetch=1,
        grid=(bsz, n_steps),
        in_specs=[per_b((1, t_new, d)), pl.BlockSpec((rows, 1), lambda b, s, pt: (0, 0)),
                  pl.BlockSpec((rows, 1), lambda b, s, pt: (0, 0)), new_spec, new_spec]
                 + [page_spec(slot) for slot in range(g)] * 2,
        out_specs=per_b((1, t_new, d)),
        scratch_shapes=[pltpu.VMEM((rows, 1), F32), pltpu.VMEM((rows, SLAB), F32)],
    )
    return pl.pallas_call(
        functools.partial(_decode_kernel, g),
        grid_spec=grid_spec,
        out_shape=jax.ShapeDtypeStruct((bsz, t_new, d), BF16),
        compiler_params=_params("parallel", "arbitrary"),
        name="attn_decode",
    )(page_table, q, bias_rows, qpos_rows, kt_new, vt_new, *([cache_k] * g), *([cache_v] * g))


def kernel(x_prompt, x_sample, cache_k, cache_v, state_pool, page_table, c_prompt, c_sample,
           w_ada, b_ada, ln_g, ln_b, w_pool, pool_scale, w_up, w_down, w_kv, w_q, w_o, b_break):
    bsz, seq, d = x_prompt.shape
    dec_b, dec_t, _ = x_sample.shape
    assert d == D_MODEL and w_ada.shape[0] == DEPTH == 2 and w_pool.shape[0] == 1 and w_q.shape[0] == 1
    past_len = page_table.shape[1] * PAGE_SIZE

    wu_bf, wd_bf = w_up.astype(BF16), w_down.astype(BF16)
    wkv_bf, wq_bf, wo_bf, wp_bf = w_kv.astype(BF16), w_q.astype(BF16), w_o.astype(BF16), w_pool.astype(BF16)
    ln_g4 = ln_g.reshape(DEPTH * 2, 1, d)
    ln_b4 = ln_b.reshape(DEPTH * 2, 1, d)

    n_tok = dec_b * dec_t
    assert n_tok % MOD_ROW_GROUP == 0
    pad_rows = lambda c: jnp.pad(c, ((0, -c.shape[0] % MOD_ROW_GROUP), (0, 0)))
    sections = [jnp.repeat(c_sample, dec_t, axis=0), pad_rows(c_prompt), pad_rows(c_sample)]
    prompt_row, seq_row = n_tok, n_tok + sections[1].shape[0]
    ada = _ada(jnp.concatenate(sections, axis=0), w_ada, b_ada)
    n_seq = MIXER_SEQS_PER_STEP
    while dec_b % n_seq:
        n_seq //= 2
    mod_p = [_ModRows(l, MOD_ROW_GROUP, lambda b: prompt_row + b, 1) for l in range(DEPTH)]
    mod_s = [_ModRows(l, n_tok, lambda b: 0, None) for l in range(DEPTH)]
    mod_seq = _ModRows(0, MOD_ROW_GROUP, lambda b: seq_row + b * n_seq, n_seq)
    xs_flat = lambda a: a.reshape(1, n_tok, d)

    zero_hist = jnp.zeros((bsz, POOL_HIST, d), F32)
    x1_p, hist_p = _mixer(x_prompt, zero_hist, ada, mod_p[0], 0, wp_bf, pool_scale, ln_g4, ln_b4, 0)
    x1_s, hist_s = _mixer(x_sample, state_pool[0], ada, mod_seq, past_len, wp_bf, pool_scale, ln_g4, ln_b4, 0)
    x2_p = _mlp(x1_p, ada, mod_p[0], wu_bf, wd_bf, ln_g4, ln_b4, 1)
    x2_s = _mlp(xs_flat(x1_s), ada, mod_s[0], wu_bf, wd_bf, ln_g4, ln_b4, 1)

    k_p, v_p, kb_p, vtb_p, q_p = _kvq(x2_p, ada, mod_p[1], wkv_bf, wq_bf, decode=False)
    k_s, v_s, kt_s, vt_s, q_s = _kvq(x2_s, ada, mod_s[1], wkv_bf, wq_bf, decode=True)

    bias = b_break[0].astype(F32)
    o_p = _attn_prompt(q_p, kb_p, vtb_p, bias)
    pages = lambda c: jnp.transpose(c, (0, 2, 3, 1)).reshape(-1, d, PAGE_SIZE)
    pad_cols = -n_tok % PAGE_SIZE
    if pad_cols:
        kt_s, vt_s = (jnp.pad(a, ((0, 0), (0, 0), (0, pad_cols))) for a in (kt_s, vt_s))
    o_s = _attn_decode(q_s.reshape(dec_b, dec_t, d), kt_s, vt_s, pages(cache_k), pages(cache_v),
                       page_table, jnp.repeat(bias, dec_t)[:, None],
                       jnp.tile(jnp.arange(dec_t, dtype=jnp.int32), N_HEADS)[:, None])

    y_p = _mlp(x2_p, ada, mod_p[1], wu_bf, wd_bf, ln_g4, ln_b4, 3, oproj=(o_p, wo_bf, 2))
    y_s = _mlp(x2_s, ada, mod_s[1], wu_bf, wd_bf, ln_g4, ln_b4, 3, oproj=(xs_flat(o_s), wo_bf, 2))

    heads = lambda a, b, t: a.reshape(b, t, N_HEADS, HEAD_DIM)
    return (y_p, y_s.reshape(dec_b, dec_t, d),
            heads(k_p, bsz, seq), heads(v_p, bsz, seq), hist_p[None],
            heads(k_s, dec_b, dec_t), heads(v_s, dec_b, dec_t), hist_s[None])
```
